```python
import math
import jax
import jax.numpy as jnp
from jax import lax
import numpy as np

D_MODEL = 1024
BATCH = 16
SEQ = 4096
DEPTH = 1

MEM_LEN = 256
NORM_EPS = 1e-5
D_FF = 2816

SSD_WIDTH = D_MODEL // 2
DIFF_WIDTH = D_MODEL - SSD_WIDTH

SSD_HEAD_DIM = 64
SSD_HEADS = SSD_WIDTH // SSD_HEAD_DIM
SSD_GROUPS = 2
SSD_STATE = 128
SSD_CONV = 5
SSD_CHUNK = 128
SSD_CONV_DIM = SSD_WIDTH + 2 * SSD_GROUPS * SSD_STATE

DIFF_HEADS = 4
DIFF_HEAD_DIM = DIFF_WIDTH // (2 * DIFF_HEADS)
DIFF_V_DIM = 2 * DIFF_HEAD_DIM
ROPE_THETA = 500000.0
ROPE_DIM = DIFF_HEAD_DIM // 4
Q_BLOCK = 128

XATTN_HEADS = 4
XATTN_HEAD_DIM = D_MODEL // XATTN_HEADS

IN_SSD = SSD_WIDTH + SSD_CONV_DIM + 2 * SSD_HEADS
IN_DIFF = 3 * DIFF_WIDTH
D_IN_PROJ = IN_SSD + IN_DIFF

kernel_name = "hybrid_ssd_diffattn_macaron_encoder"


def rms_norm(x, g):
    xf = x.astype(jnp.float32)
    y = xf * lax.rsqrt(jnp.mean(xf * xf, axis=-1, keepdims=True) + NORM_EPS)
    return (y * g.astype(jnp.float32)).astype(x.dtype)


def swiglu(h, w_gate, w_up, w_down):
    return (jax.nn.silu(h @ w_gate) * (h @ w_up)) @ w_down


def rope_tables(seq_len):
    pos = jnp.arange(seq_len, dtype=jnp.float32)
    inv_freq = 1.0 / (ROPE_THETA ** (jnp.arange(0, ROPE_DIM, 2, dtype=jnp.float32) / ROPE_DIM))
    ang = pos[:, None] * inv_freq[None, :]
    return jnp.cos(ang), jnp.sin(ang)


def partial_rope(t, cos, sin):
    half = ROPE_DIM // 2
    c = cos[None, :, None, None, :]
    s_ = sin[None, :, None, None, :]
    tf = t[..., :ROPE_DIM].astype(jnp.float32)
    t1, t2 = tf[..., :half], tf[..., half:]
    rot = jnp.concatenate([t1 * c - t2 * s_, t2 * c + t1 * s_], axis=-1).astype(t.dtype)
    return jnp.concatenate([rot, t[..., ROPE_DIM:]], axis=-1)


def segment_decay(a_cum):
    L = a_cum.shape[2]
    diff = a_cum[:, :, :, None] - a_cum[:, :, None, :]
    mask = jnp.tril(jnp.ones((L, L), dtype=bool))[None, None, :, :, None, None]
    return jnp.exp(jnp.where(mask, diff, -jnp.inf))


def ssd_chunked(xh, dt, A, Bg, Cg):
    b, s, h, p = xh.shape
    g, n = Bg.shape[-2], Bg.shape[-1]
    r = h // g
    L = SSD_CHUNK
    c = s // L
    a = (dt * A).reshape(b, c, L, g, r)
    xdt = (xh.astype(jnp.float32) * dt[..., None]).reshape(b, c, L, g, r, p)
    Bc = Bg.astype(jnp.float32).reshape(b, c, L, g, n)
    Cc = Cg.astype(jnp.float32).reshape(b, c, L, g, n)
    a_cum = jnp.cumsum(a, axis=2)
    cb = jnp.einsum("bclgn,bcsgn->bclsg", Cc, Bc)
    y_diag = jnp.einsum("bclsg,bclsgr,bcsgrp->bclgrp", cb, segment_decay(a_cum), xdt)
    decay_to_end = jnp.exp(a_cum[:, :, -1:] - a_cum)
    states = jnp.einsum("bclgn,bclgr,bclgrp->bcgrpn", Bc, decay_to_end, xdt)
    chunk_decay = jnp.exp(a_cum[:, :, -1])

    def step(carry, inp):
        st, dec = inp
        return carry * dec[..., None, None] + st, carry

    _, prev = lax.scan(step, jnp.zeros_like(states[:, 0]),
                       (jnp.moveaxis(states, 1, 0), jnp.moveaxis(chunk_decay, 1, 0)))
    prev = jnp.moveaxis(prev, 0, 1)
    y_off = jnp.einsum("bclgn,bcgrpn,bclgr->bclgrp", Cc, prev, jnp.exp(a_cum))
    return (y_diag + y_off).reshape(b, s, h, p)


def ssd_mixer(zxbcdt, conv_w, conv_b, dt_bias_fwd, dt_bias_bwd, A_log_fwd, A_log_bwd, D_skip, norm_g):
    b, s, _ = zxbcdt.shape
    z = zxbcdt[..., :SSD_WIDTH]
    xbc = zxbcdt[..., SSD_WIDTH:SSD_WIDTH + SSD_CONV_DIM]
    dt_raw = zxbcdt[..., SSD_WIDTH + SSD_CONV_DIM:].astype(jnp.float32)
    xbc = lax.conv_general_dilated(
        xbc, conv_w[:, None, :], window_strides=(1,),
        padding=[(SSD_CONV // 2, SSD_CONV // 2)],
        dimension_numbers=("NWC", "WIO", "NWC"),
        feature_group_count=SSD_CONV_DIM) + conv_b
    xbc = jax.nn.silu(xbc)
    xs = xbc[..., :SSD_WIDTH].reshape(b, s, SSD_HEADS, SSD_HEAD_DIM)
    Bg = xbc[..., SSD_WIDTH:SSD_WIDTH + SSD_GROUPS * SSD_STATE].reshape(b, s, SSD_GROUPS, SSD_STATE)
    Cg = xbc[..., SSD_WIDTH + SSD_GROUPS * SSD_STATE:].reshape(b, s, SSD_GROUPS, SSD_STATE)
    dt_f = jax.nn.softplus(dt_raw[..., :SSD_HEADS] + dt_bias_fwd.astype(jnp.float32))
    dt_b = jax.nn.softplus(dt_raw[..., SSD_HEADS:] + dt_bias_bwd.astype(jnp.float32))
    A_f = -jnp.exp(A_log_fwd.astype(jnp.float32))
    A_b = -jnp.exp(A_log_bwd.astype(jnp.float32))
    flip = lambda t: jnp.flip(t, axis=1)
    y_f = ssd_chunked(xs, dt_f, A_f, Bg, Cg)
    y_b = flip(ssd_chunked(flip(xs), flip(dt_b), A_b, flip(Bg), flip(Cg)))
    y = y_f + y_b + D_skip.astype(jnp.float32)[:, None] * xs.astype(jnp.float32)
    y = y.reshape(b, s, SSD_WIDTH) * jax.nn.silu(z.astype(jnp.float32))
    y = rms_norm(y.reshape(b, s, SSD_GROUPS, SSD_WIDTH // SSD_GROUPS),
                 norm_g.reshape(SSD_GROUPS, SSD_WIDTH // SSD_GROUPS))
    return y.reshape(b, s, SSD_WIDTH).astype(zxbcdt.dtype)


def diff_attention(qkv, lambda_q1, lambda_k1, lambda_q2, lambda_k2, subln_g, lambda_init):
    b, s, _ = qkv.shape
    H, d = DIFF_HEADS, DIFF_HEAD_DIM
    q = qkv[..., :DIFF_WIDTH].reshape(b, s, H, 2, d)
    k = qkv[..., DIFF_WIDTH:2 * DIFF_WIDTH].reshape(b, s, H, 2, d)
    v = qkv[..., 2 * DIFF_WIDTH:].reshape(b, s, H, DIFF_V_DIM)
    cos, sin = rope_tables(s)
    q = partial_rope(q, cos, sin) * (d ** -0.5)
    k = partial_rope(k, cos, sin)
    f32 = jnp.float32
    lam = (jnp.exp(jnp.sum(lambda_q1.astype(f32) * lambda_k1.astype(f32)))
           - jnp.exp(jnp.sum(lambda_q2.astype(f32) * lambda_k2.astype(f32))) + lambda_init)
    qb = jnp.moveaxis(q.reshape(b, s // Q_BLOCK, Q_BLOCK, H, 2, d), 1, 0)

    def block(q_blk):
        sc = jnp.einsum("bqhtd,bkhtd->bhtqk", q_blk, k).astype(f32)
        pr = jax.nn.softmax(sc, axis=-1)
        a = pr[:, :, 0] - lam * pr[:, :, 1]
        return jnp.einsum("bhqk,bkhe->bqhe", a, v.astype(f32)).astype(v.dtype)

    o = lax.map(block, qb)
    o = jnp.moveaxis(o, 0, 1).reshape(b, s, H, DIFF_V_DIM)
    o = rms_norm(o, subln_g) * (1.0 - lambda_init)
    return o.reshape(b, s, DIFF_WIDTH).astype(qkv.dtype)


def memory_cross_attention(h, mem_h, w_q, w_kv, w_o):
    b, s, _ = h.shape
    m = mem_h.shape[1]
    q = (h @ w_q).reshape(b, s, XATTN_HEADS, XATTN_HEAD_DIM)
    kv = (mem_h @ w_kv).reshape(b, m, 2, XATTN_HEADS, XATTN_HEAD_DIM)
    k, v = kv[:, :, 0], kv[:, :, 1]
    sc = jnp.einsum("bqhd,bkhd->bhqk", q, k).astype(jnp.float32) * (XATTN_HEAD_DIM ** -0.5)
    pr = jax.nn.softmax(sc, axis=-1)
    o = jnp.einsum("bhqk,bkhd->bqhd", pr, v.astype(jnp.float32)).astype(h.dtype)
    return o.reshape(b, s, D_MODEL) @ w_o


def setup_inputs(seed: int = 0) -> dict:
    key = jax.random.key(seed)
    keys = iter(jax.random.split(key, 40))
    f32 = jnp.float32

    def normal(shape, scale):
        return jax.random.normal(next(keys), shape, dtype=f32) * scale

    def gain(shape):
        return 1.0 + normal(shape, 0.02)

    x = normal((BATCH, SEQ, D_MODEL), 1.0)
    mem = normal((BATCH, MEM_LEN, D_MODEL), 1.0)
    dt0_f = jnp.exp(jax.random.uniform(next(keys), (DEPTH, SSD_HEADS), minval=math.log(1e-3), maxval=math.log(1e-1)))
    dt0_b = jnp.exp(jax.random.uniform(next(keys), (DEPTH, SSD_HEADS), minval=math.log(1e-3), maxval=math.log(1e-1)))
    return {
        "x": x,
        "mem": mem,
        "ffn1_norm_g": gain((DEPTH, D_MODEL)),
        "ffn1_w_gate": normal((DEPTH, D_MODEL, D_FF), D_MODEL ** -0.5),
        "ffn1_w_up": normal((DEPTH, D_MODEL, D_FF), D_MODEL ** -0.5),
        "ffn1_w_down": normal((DEPTH, D_FF, D_MODEL), D_FF ** -0.5),
        "mix_norm_g": gain((DEPTH, D_MODEL)),
        "w_in": normal((DEPTH, D_MODEL, D_IN_PROJ), D_MODEL ** -0.5),
        "conv_w": normal((DEPTH, SSD_CONV, SSD_CONV_DIM), SSD_CONV ** -0.5),
        "conv_b": normal((DEPTH, SSD_CONV_DIM), 0.01),
        "dt_bias_fwd": dt0_f + jnp.log(-jnp.expm1(-dt0_f)),
        "dt_bias_bwd": dt0_b + jnp.log(-jnp.expm1(-dt0_b)),
        "A_log_fwd": jnp.log(jax.random.uniform(next(keys), (DEPTH, SSD_HEADS), minval=1.0, maxval=16.0)),
        "A_log_bwd": jnp.log(jax.random.uniform(next(keys), (DEPTH, SSD_HEADS), minval=1.0, maxval=16.0)),
        "D_skip": 1.0 + normal((DEPTH, SSD_HEADS), 0.1),
        "ssd_norm_g": gain((DEPTH, SSD_WIDTH)),
        "lambda_q1": normal((DEPTH, DIFF_HEAD_DIM), 0.1),
        "lambda_k1": normal((DEPTH, DIFF_HEAD_DIM), 0.1),
        "lambda_q2": normal((DEPTH, DIFF_HEAD_DIM), 0.1),
        "lambda_k2": normal((DEPTH, DIFF_HEAD_DIM), 0.1),
        "diff_subln_g": gain((DEPTH, DIFF_V_DIM)),
        "w_out": normal((DEPTH, D_MODEL, D_MODEL), D_MODEL ** -0.5),
        "xattn_norm_g": gain((DEPTH, D_MODEL)),
        "mem_norm_g": gain((DEPTH, D_MODEL)),
        "xattn_w_q": normal((DEPTH, D_MODEL, D_MODEL), D_MODEL ** -0.5),
        "xattn_w_kv": normal((DEPTH, D_MODEL, 2 * D_MODEL), D_MODEL ** -0.5),
        "xattn_w_o": normal((DEPTH, D_MODEL, D_MODEL), D_MODEL ** -0.5),
        "ffn2_norm_g": gain((DEPTH, D_MODEL)),
        "ffn2_w_gate": normal((DEPTH, D_MODEL, D_FF), D_MODEL ** -0.5),
        "ffn2_w_up": normal((DEPTH, D_MODEL, D_FF), D_MODEL ** -0.5),
        "ffn2_w_down": normal((DEPTH, D_FF, D_MODEL), D_FF ** -0.5),
        "final_norm_g": gain((D_MODEL,)),
    }


def reference(x, mem, ffn1_norm_g, ffn1_w_gate, ffn1_w_up, ffn1_w_down, mix_norm_g, w_in,
              conv_w, conv_b, dt_bias_fwd, dt_bias_bwd, A_log_fwd, A_log_bwd, D_skip, ssd_norm_g,
              lambda_q1, lambda_k1, lambda_q2, lambda_k2, diff_subln_g, w_out,
              xattn_norm_g, mem_norm_g, xattn_w_q, xattn_w_kv, xattn_w_o,
              ffn2_norm_g, ffn2_w_gate, ffn2_w_up, ffn2_w_down, final_norm_g):
    for layer in range(DEPTH):
        lambda_init = 0.8 - 0.6 * math.exp(-0.3 * layer)
        h = rms_norm(x, ffn1_norm_g[layer])
        x = x + 0.5 * swiglu(h, ffn1_w_gate[layer], ffn1_w_up[layer], ffn1_w_down[layer])
        h = rms_norm(x, mix_norm_g[layer])
        proj = h @ w_in[layer]
        y_ssd = ssd_mixer(proj[..., :IN_SSD], conv_w[layer], conv_b[layer],
                          dt_bias_fwd[layer], dt_bias_bwd[layer], A_log_fwd[layer], A_log_bwd[layer],
                          D_skip[layer], ssd_norm_g[layer])
        y_diff = diff_attention(proj[..., IN_SSD:], lambda_q1[layer], lambda_k1[layer],
                                lambda_q2[layer], lambda_k2[layer], diff_subln_g[layer], lambda_init)
        y_mix = jnp.concatenate([y_ssd.astype(x.dtype), y_diff.astype(x.dtype)], axis=-1)
        x = x + y_mix @ w_out[layer]
        h = rms_norm(x, xattn_norm_g[layer])
        m = rms_norm(mem, mem_norm_g[layer])
        x = x + memory_cross_attention(h, m, xattn_w_q[layer], xattn_w_kv[layer], xattn_w_o[layer])
        h = rms_norm(x, ffn2_norm_g[layer])
        x = x + 0.5 * swiglu(h, ffn2_w_gate[layer], ffn2_w_up[layer], ffn2_w_down[layer])
    return rms_norm(x, final_norm_g)
```

```python
import functools
import math

import jax
import jax.numpy as jnp
from jax import lax
from jax.experimental import pallas as pl
from jax.experimental.pallas import tpu as pltpu

F32 = jnp.float32
BF16 = jnp.bfloat16

NORM_EPS = 1e-5
D_MODEL = 1024
D_FF = 2816
FF_CHUNK = 256
N_FF_CHUNKS = D_FF // FF_CHUNK
SSD_WIDTH = 512
SSD_HEADS = 8
SSD_HEAD_DIM = 64
SSD_GROUPS = 2
SSD_STATE = 128
SSD_CONV = 5
SSD_CHUNK = 128
SSD_CONV_DIM = 1024
DIFF_WIDTH = 512
DIFF_HEADS = 4
DIFF_HEAD_DIM = 64
DIFF_V_DIM = 128
ROPE_THETA = 500000.0
ROPE_DIM = 16
XATTN_HEADS = 4
XATTN_HEAD_DIM = 256
LANES = 128
SUBLANES = 8
DT_PAD = LANES
IN_Z, IN_XBC, IN_DT = 0, SSD_WIDTH, SSD_WIDTH + SSD_CONV_DIM
IN_Q = IN_DT + DT_PAD
IN_K = IN_Q + DIFF_WIDTH
IN_V = IN_K + DIFF_WIDTH
IN_TOTAL = IN_V + DIFF_WIDTH
VMEM_LIMIT = 56 * 1024 * 1024


def _dot(a, b):
    return jnp.dot(a, b, preferred_element_type=F32)


def _rms(xf, g):
    ms = jnp.sum(xf * xf, axis=-1, keepdims=True) * (1.0 / xf.shape[-1])
    return xf * lax.rsqrt(ms + NORM_EPS) * g


def _silu(x):
    return x * jax.nn.sigmoid(x)


def _swiglu_residual(x, g_ref, wg_ref, wu_ref, wd_ref, acc_ref):
    h = _rms(x, g_ref[...]).astype(BF16)
    acc_ref[...] = jnp.zeros_like(acc_ref)

    def body(c, carry):
        gate = _dot(h, wg_ref[c])
        up = _dot(h, wu_ref[c])
        act = (_silu(gate) * up).astype(BF16)
        acc_ref[...] += _dot(act, wd_ref[c])
        return carry

    lax.fori_loop(0, N_FF_CHUNKS, body, 0)
    return x + 0.5 * acc_ref[...]


def _const_spec(shape):
    nd = len(shape)
    return pl.BlockSpec(shape, lambda *_: (0,) * nd, pipeline_mode=pl.Buffered(1))


def _kv_kernel(mem_ref, g_ref, wkt_ref, wv_ref, kt_ref, v_ref):
    m = _rms(mem_ref[...], g_ref[...]).astype(BF16)
    kt = lax.dot_general(wkt_ref[...], m, (((1,), (1,)), ((), ())),
                         preferred_element_type=F32)
    kt_ref[...] = kt.astype(BF16)
    v_ref[...] = _dot(m, wv_ref[...]).astype(BF16)


def _kv_call(mem, g, wkt, wv):
    b, m, d = mem.shape
    return pl.pallas_call(
        _kv_kernel,
        grid=(b,),
        in_specs=[pl.BlockSpec((None, m, d), lambda i: (i, 0, 0)),
                  _const_spec((1, d)), _const_spec((d, d)), _const_spec((d, d))],
        out_specs=[pl.BlockSpec((None, d, m), lambda i: (i, 0, 0)),
                   pl.BlockSpec((None, m, d), lambda i: (i, 0, 0))],
        out_shape=[jax.ShapeDtypeStruct((b, d, m), BF16), jax.ShapeDtypeStruct((b, m, d), BF16)],
        compiler_params=pltpu.CompilerParams(dimension_semantics=("arbitrary",),
                                             vmem_limit_bytes=VMEM_LIMIT),
        name="kv",
    )(mem, g, wkt, wv)


def _rope(t, cos, sa, sb):
    return t * cos + pltpu.roll(t, 8, axis=1) * sa + pltpu.roll(t, LANES - 8, axis=1) * sb


def _ffn_inproj_kernel(x_ref, g1_ref, wg_ref, wu_ref, wd_ref, gmix_ref, win_ref,
                       cos_ref, sa_ref, sb_ref,
                       x1_ref, z_ref, xbc_ref, dt_ref, q_ref, k_ref, v_ref, acc_ref):
    x1 = _swiglu_residual(x_ref[...], g1_ref, wg_ref, wu_ref, wd_ref, acc_ref)
    x1_ref[...] = x1
    h = _rms(x1, gmix_ref[...]).astype(BF16)
    z_ref[...] = _dot(h, win_ref[:, IN_Z:IN_XBC])
    xbc_ref[...] = _dot(h, win_ref[:, IN_XBC:IN_DT])
    dt_ref[...] = _dot(h, win_ref[:, IN_DT:IN_Q])
    v_ref[...] = _dot(h, win_ref[:, IN_V:IN_TOTAL]).astype(BF16)
    cos, sa, sb = cos_ref[...], sa_ref[...], sb_ref[...]
    q = _dot(h, win_ref[:, IN_Q:IN_K]) * (DIFF_HEAD_DIM ** -0.5)
    k = _dot(h, win_ref[:, IN_K:IN_V])
    for s in range(DIFF_WIDTH // LANES):
        sl = slice(s * LANES, (s + 1) * LANES)
        q_ref[:, sl] = _rope(q[:, sl], cos, sa, sb).astype(BF16)
        k_ref[:, sl] = _rope(k[:, sl], cos, sa, sb).astype(BF16)


def _ffn_inproj_call(xf, g1, wg, wu, wd, gmix, win, cos_t, sa_t, sb_t, seq, tm):
    t, d = xf.shape
    nseq = seq // tm
    row = lambda w: pl.BlockSpec((tm, w), lambda i: (i, 0))
    tab = pl.BlockSpec((tm, LANES), lambda i: (i % nseq, 0))
    return pl.pallas_call(
        _ffn_inproj_kernel,
        grid=(t // tm,),
        in_specs=[row(d), _const_spec((1, d)),
                  _const_spec(wg.shape), _const_spec(wu.shape), _const_spec(wd.shape),
                  _const_spec((1, d)), _const_spec(win.shape), tab, tab, tab],
        out_specs=[row(d), row(SSD_WIDTH), row(SSD_CONV_DIM), row(DT_PAD),
                   row(DIFF_WIDTH), row(DIFF_WIDTH), row(DIFF_WIDTH)],
        out_shape=[jax.ShapeDtypeStruct((t, d), F32),
                   jax.ShapeDtypeStruct((t, SSD_WIDTH), F32),
                   jax.ShapeDtypeStruct((t, SSD_CONV_DIM), F32),
                   jax.ShapeDtypeStruct((t, DT_PAD), F32),
                   jax.ShapeDtypeStruct((t, DIFF_WIDTH), BF16),
                   jax.ShapeDtypeStruct((t, DIFF_WIDTH), BF16),
                   jax.ShapeDtypeStruct((t, DIFF_WIDTH), BF16)],
        scratch_shapes=[pltpu.VMEM((tm, d), F32)],
        compiler_params=pltpu.CompilerParams(dimension_semantics=("arbitrary",),
                                             vmem_limit_bytes=VMEM_LIMIT),
        name="ffn_inproj",
    )(xf, g1, wg, wu, wd, gmix, win, cos_t, sa_t, sb_t)


def _split_dot(tri, a):
    a0 = a.astype(BF16)
    r1 = a - a0.astype(F32)
    a1 = r1.astype(BF16)
    a2 = (r1 - a1.astype(F32)).astype(BF16)
    return _dot(tri, a0) + _dot(tri, a1) + _dot(tri, a2)


def _pair_expand(tile, c0):
    rows = tile.shape[0]
    lane = lax.broadcasted_iota(jnp.int32, (rows, LANES), 1)
    lo = jnp.broadcast_to(tile[:, c0:c0 + 1], (rows, LANES))
    hi = jnp.broadcast_to(tile[:, c0 + 1:c0 + 2], (rows, LANES))
    return jnp.where(lane < SSD_HEAD_DIM, lo, hi)


def _ssd_chunk(xc, dtraw, dtbias, a_neg, state_ref, d, with_skip, dexp):
    L = SSD_CHUNK
    row = lax.broadcasted_iota(jnp.int32, (L, L), 0)
    col = lax.broadcasted_iota(jnp.int32, (L, L), 1)
    xs = xc[:, :SSD_WIDTH]
    v = dtraw + dtbias
    dt = jnp.maximum(v, 0.0) + jnp.log1p(jnp.exp(-jnp.abs(v)))
    a = dt * a_neg
    if d == 0:
        valid = row >= col
    else:
        valid = row <= col
    acum = _split_dot(jnp.where(valid, 1.0, 0.0).astype(BF16), a)
    tot = acum[L - 1:L, :] if d == 0 else acum[0:1, :]
    eac = jnp.exp(acum)
    wl = dt * jnp.exp(tot - acum)
    cdec = jnp.exp(tot)
    acum_t = acum.T
    dt_t = dt.T
    lane = lax.broadcasted_iota(jnp.int32, (L, LANES), 1)
    lane2 = lax.broadcasted_iota(jnp.int32, (2 * L, LANES), 1)
    row2 = lax.broadcasted_iota(jnp.int32, (2 * L, LANES), 0)
    bd_mask = jnp.right_shift(row2, 7) == jnp.right_shift(lane2, 6)
    ys = []
    for g in range(SSD_GROUPS):
        bg = xc[:, SSD_WIDTH + g * SSD_STATE:SSD_WIDTH + (g + 1) * SSD_STATE]
        cg = xc[:, SSD_WIDTH + (SSD_GROUPS + g) * SSD_STATE:SSD_WIDTH + (SSD_GROUPS + g + 1) * SSD_STATE]
        bt = bg.T.astype(BF16)
        cb16 = cg.astype(BF16)
        cb = _dot(cb16, bt)
        prev = state_ref[d, g]
        yoff = _dot(cb16, prev.astype(BF16))
        w_parts, cd_parts = [], []
        for p in range(2):
            h0 = 4 * g + 2 * p
            c0 = SSD_HEADS * d + h0
            ms = []
            for c in (c0, c0 + 1):
                diff = acum[:, c:c + 1] - acum_t[c:c + 1, :]
                dec = jnp.exp(jnp.where(valid, diff, -jnp.inf))
                ms.append((cb * dec * dt_t[c:c + 1, :]).astype(BF16))
            lhs = jnp.concatenate(ms, axis=1)
            xp = xs[:, h0 * SSD_HEAD_DIM:(h0 + 2) * SSD_HEAD_DIM]
            xp2 = jnp.concatenate([xp, xp], axis=0)
            rhs = jnp.where(bd_mask, xp2, 0.0).astype(BF16)
            ydiag = _dot(lhs, rhs)
            y = ydiag + yoff[:, p * LANES:(p + 1) * LANES] * _pair_expand(eac, c0)
            if with_skip:
                y = y + dexp[:, h0 * SSD_HEAD_DIM:(h0 + 2) * SSD_HEAD_DIM] * xp
            ys.append(y)
            w_parts.append((xp * _pair_expand(wl, c0)).astype(BF16))
            cd_parts.append(_pair_expand(cdec, c0))
        w = jnp.concatenate(w_parts, axis=1)
        cd = jnp.concatenate(cd_parts, axis=1)
        state_ref[d, g] = prev * cd + _dot(bt, w)
    return jnp.concatenate(ys, axis=1)


def _ssd_kernel(xf_ref, xfp_ref, xfn_ref, xb_ref, xbp_ref, xbn_ref, dtf_ref, dtb_ref,
                cw_ref, cbias_ref, dtbias_ref, alog_ref, dexp_ref,
                yf_ref, yb_ref, state_ref, pad_ref, *, rc):
    j = pl.program_id(1)
    nb = pl.num_programs(1)
    rows = rc * SSD_CHUNK

    @pl.when(j == 0)
    def _():
        state_ref[...] = jnp.zeros_like(state_ref)

    a_neg = -jnp.exp(alog_ref[...])
    dtbias = dtbias_ref[...]
    dexp = dexp_ref[...]
    cw = cw_ref[...]
    cbias = cbias_ref[...]

    def conv(x_ref, prev_ref, next_ref, blk):
        pad_ref[0:SUBLANES, :] = jnp.where(blk > 0, prev_ref[...], 0.0)
        pad_ref[SUBLANES:SUBLANES + rows, :] = x_ref[...]
        pad_ref[SUBLANES + rows:2 * SUBLANES + rows, :] = jnp.where(blk < nb - 1, next_ref[...], 0.0)

    def conv_chunk(c):
        base = SUBLANES + c * SSD_CHUNK - SSD_CONV // 2
        acc = cbias
        for k in range(SSD_CONV):
            acc = acc + pad_ref[base + k:base + k + SSD_CHUNK, :] * cw[k:k + 1, :]
        return _silu(acc)

    conv(xf_ref, xfp_ref, xfn_ref, j)
    for c in range(rc):
        sl = slice(c * SSD_CHUNK, (c + 1) * SSD_CHUNK)
        yf_ref[sl, :] = _ssd_chunk(conv_chunk(c), dtf_ref[sl, :], dtbias, a_neg, state_ref, 0, True, dexp)
    conv(xb_ref, xbp_ref, xbn_ref, nb - 1 - j)
    for c in reversed(range(rc)):
        sl = slice(c * SSD_CHUNK, (c + 1) * SSD_CHUNK)
        yb_ref[sl, :] = _ssd_chunk(conv_chunk(c), dtb_ref[sl, :], dtbias, a_neg, state_ref, 1, False, dexp)


def _ssd_call(xbc, dt, cw, cbias, dtbias, alog, dexp, rc):
    b, s, cdim = xbc.shape
    rows = rc * SSD_CHUNK
    nb = s // rows
    hb = rows // SUBLANES
    nh = s // SUBLANES
    main = lambda w, rev: pl.BlockSpec(
        (None, rows, w), (lambda i, j: (i, nb - 1 - j, 0)) if rev else (lambda i, j: (i, j, 0)))

    def halo(rev, nxt):
        def idx(i, j):
            blk = nb - 1 - j if rev else j
            h = (blk + 1) * hb if nxt else blk * hb - 1
            return (i, jnp.clip(h, 0, nh - 1), 0)
        return pl.BlockSpec((None, SUBLANES, cdim), idx)

    return pl.pallas_call(
        functools.partial(_ssd_kernel, rc=rc),
        grid=(b, nb),
        in_specs=[main(cdim, False), halo(False, False), halo(False, True),
                  main(cdim, True), halo(True, False), halo(True, True),
                  main(DT_PAD, False), main(DT_PAD, True),
                  _const_spec(cw.shape), _const_spec(cbias.shape), _const_spec(dtbias.shape),
                  _const_spec(alog.shape), _const_spec(dexp.shape)],
        out_specs=[main(SSD_WIDTH, False), main(SSD_WIDTH, True)],
        out_shape=[jax.ShapeDtypeStruct((b, s, SSD_WIDTH), F32),
                   jax.ShapeDtypeStruct((b, s, SSD_WIDTH), F32)],
        scratch_shapes=[pltpu.VMEM((2, SSD_GROUPS, SSD_STATE, 4 * SSD_HEAD_DIM), F32),
                        pltpu.VMEM((rows + 2 * SUBLANES, cdim), F32)],
        compiler_params=pltpu.CompilerParams(dimension_semantics=("arbitrary", "arbitrary"),
                                             vmem_limit_bytes=VMEM_LIMIT),
        name="ssd",
    )(xbc, xbc, xbc, xbc, xbc, xbc, dt, dt, cw, cbias, dtbias, alog, dexp)


def _diffattn_kernel(q_ref, k_ref, v_ref, lam_ref, g_ref, o_ref, *, lambda_init):
    q = q_ref[...]
    k = k_ref[...]
    lamv = lam_ref[...]
    l1 = jnp.sum(lamv[0:1] * lamv[1:2], axis=-1, keepdims=True)
    l2 = jnp.sum(lamv[2:3] * lamv[3:4], axis=-1, keepdims=True)
    lam = jnp.exp(l1) - jnp.exp(l2) + lambda_init
    lane = lax.broadcasted_iota(jnp.int32, q.shape, 1)
    probs = []
    for t in range(2):
        half = lane < DIFF_HEAD_DIM if t == 0 else lane >= DIFF_HEAD_DIM
        qt = jnp.where(half, q, jnp.zeros_like(q))
        s = lax.dot_general(qt, k, (((1,), (1,)), ((), ())), preferred_element_type=F32)
        m = jnp.max(s, axis=-1, keepdims=True)
        p = jnp.exp(s - m)
        probs.append(p * (1.0 / jnp.sum(p, axis=-1, keepdims=True)))
    a = (probs[0] - lam * probs[1]).astype(BF16)
    o = _dot(a, v_ref[...])
    o_ref[...] = (_rms(o, g_ref[...]) * (1.0 - lambda_init)).astype(o_ref.dtype)


def _diffattn_call(q, k, v, lamv, g, lambda_init, tq):
    b, s, _ = q.shape
    return pl.pallas_call(
        functools.partial(_diffattn_kernel, lambda_init=lambda_init),
        grid=(b, DIFF_HEADS, s // tq),
        in_specs=[pl.BlockSpec((None, tq, DIFF_V_DIM), lambda i, h, j: (i, j, h)),
                  pl.BlockSpec((None, s, DIFF_V_DIM), lambda i, h, j: (i, 0, h)),
                  pl.BlockSpec((None, s, DIFF_V_DIM), lambda i, h, j: (i, 0, h)),
                  _const_spec(lamv.shape), _const_spec(g.shape)],
        out_specs=pl.BlockSpec((None, tq, DIFF_V_DIM), lambda i, h, j: (i, j, h)),
        out_shape=jax.ShapeDtypeStruct((b, s, DIFF_WIDTH), BF16),
        compiler_params=pltpu.CompilerParams(
            dimension_semantics=("arbitrary", "arbitrary", "arbitrary"),
            vmem_limit_bytes=VMEM_LIMIT),
        name="diffattn",
    )(q, k, v, lamv, g)


def _tail_kernel(x1_ref, yf_ref, yb_ref, z_ref, yd_ref, kt_ref, vm_ref,
                 gssd_ref, wout_ref, gx_ref, wq_ref, wo_ref,
                 g2_ref, wg_ref, wu_ref, wd_ref, gfin_ref,
                 o_ref, acc_ref):
    y = (yf_ref[...] + yb_ref[...]) * _silu(z_ref[...])
    gw = SSD_WIDTH // SSD_GROUPS
    gssd = gssd_ref[...]
    yn = jnp.concatenate(
        [_rms(y[:, g * gw:(g + 1) * gw], gssd[:, g * gw:(g + 1) * gw]) for g in range(SSD_GROUPS)],
        axis=1).astype(BF16)
    x2 = x1_ref[...] + _dot(yn, wout_ref[0:SSD_WIDTH, :]) + _dot(yd_ref[...], wout_ref[SSD_WIDTH:, :])
    h = _rms(x2, gx_ref[...]).astype(BF16)
    qx = (_dot(h, wq_ref[...]) * (XATTN_HEAD_DIM ** -0.5)).astype(BF16)
    heads = []
    for hd in range(XATTN_HEADS):
        sl = slice(hd * XATTN_HEAD_DIM, (hd + 1) * XATTN_HEAD_DIM)
        s = _dot(qx[:, sl], kt_ref[sl, :])
        p = jnp.exp(s - jnp.max(s, axis=-1, keepdims=True))
        p = p * (1.0 / jnp.sum(p, axis=-1, keepdims=True))
        heads.append(_dot(p.astype(BF16), vm_ref[:, sl]).astype(BF16))
    x3 = x2 + _dot(jnp.concatenate(heads, axis=1), wo_ref[...])
    x4 = _swiglu_residual(x3, g2_ref, wg_ref, wu_ref, wd_ref, acc_ref)
    o_ref[...] = _rms(x4, gfin_ref[...])


def _tail_call(x1, yf, yb, z, yd, kt, vm, gssd, wout, gx, wq, wo, g2, wg, wu, wd, gfin, seq, tm):
    t, d = x1.shape
    nseq = seq // tm
    m = vm.shape[1]
    row = lambda w: pl.BlockSpec((tm, w), lambda i: (i, 0))
    return pl.pallas_call(
        _tail_kernel,
        grid=(t // tm,),
        in_specs=[row(d), row(SSD_WIDTH), row(SSD_WIDTH), row(SSD_WIDTH), row(DIFF_WIDTH),
                  pl.BlockSpec((None, d, m), lambda i: (i // nseq, 0, 0)),
                  pl.BlockSpec((None, m, d), lambda i: (i // nseq, 0, 0)),
                  _const_spec(gssd.shape), _const_spec(wout.shape), _const_spec(gx.shape),
                  _const_spec(wq.shape), _const_spec(wo.shape), _const_spec(g2.shape),
                  _const_spec(wg.shape), _const_spec(wu.shape), _const_spec(wd.shape),
                  _const_spec(gfin.shape)],
        out_specs=row(d),
        out_shape=jax.ShapeDtypeStruct((t, d), F32),
        scratch_shapes=[pltpu.VMEM((tm, d), F32)],
        compiler_params=pltpu.CompilerParams(dimension_semantics=("arbitrary",),
                                             vmem_limit_bytes=VMEM_LIMIT),
        name="tail",
    )(x1, yf, yb, z, yd, kt, vm, gssd, wout, gx, wq, wo, g2, wg, wu, wd, gfin)


def _rope_lane_tables(seq):
    pos = jnp.arange(seq, dtype=F32)
    inv_freq = 1.0 / (ROPE_THETA ** (jnp.arange(0, ROPE_DIM, 2, dtype=F32) / ROPE_DIM))
    ang = pos[:, None] * inv_freq[None, :]
    cos, sin = jnp.cos(ang), jnp.sin(ang)
    half = ROPE_DIM // 2
    pad = jnp.zeros((seq, DIFF_HEAD_DIM - ROPE_DIM), F32)
    zeros = jnp.zeros((seq, half), F32)
    cos64 = jnp.concatenate([cos, cos, pad + 1.0], axis=1)
    sa64 = jnp.concatenate([zeros, sin, pad], axis=1)
    sb64 = jnp.concatenate([-sin, zeros, pad], axis=1)
    tile = lambda a: jnp.concatenate([a, a], axis=1)
    return tile(cos64), tile(sa64), tile(sb64)


def _ff_chunks(w_gate, w_up, w_down):
    d = w_gate.shape[0]
    wg = w_gate.astype(BF16).reshape(d, N_FF_CHUNKS, FF_CHUNK).transpose(1, 0, 2)
    wu = w_up.astype(BF16).reshape(d, N_FF_CHUNKS, FF_CHUNK).transpose(1, 0, 2)
    wd = w_down.astype(BF16).reshape(N_FF_CHUNKS, FF_CHUNK, d)
    return wg, wu, wd


def _pad_lanes(v, fill):
    return jnp.concatenate([v.astype(F32), jnp.full((LANES - v.shape[0],), fill, F32)])[None, :]


def kernel(x, mem, ffn1_norm_g, ffn1_w_gate, ffn1_w_up, ffn1_w_down, mix_norm_g, w_in, conv_w, conv_b, dt_bias_fwd, dt_bias_bwd, A_log_fwd, A_log_bwd, D_skip, ssd_norm_g, lambda_q1, lambda_k1, lambda_q2, lambda_k2, diff_subln_g, w_out, xattn_norm_g, mem_norm_g, xattn_w_q, xattn_w_kv, xattn_w_o, ffn2_norm_g, ffn2_w_gate, ffn2_w_up, ffn2_w_down, final_norm_g):
    b, s, d = x.shape
    depth = w_in.shape[0]
    assert depth == 1, "single-layer block only"
    tm = 512
    tq = 256
    rc = 1
    cos_t, sa_t, sb_t = _rope_lane_tables(s)
    xf = x.reshape(b * s, d)
    row = lambda v: v.astype(F32)[None, :]
    for layer in range(depth):
        lambda_init = 0.8 - 0.6 * math.exp(-0.3 * layer)
        w_kv = xattn_w_kv[layer]
        kt, vm = _kv_call(mem, row(mem_norm_g[layer]),
                          w_kv[:, :d].T.astype(BF16), w_kv[:, d:].astype(BF16))
        wg1, wu1, wd1 = _ff_chunks(ffn1_w_gate[layer], ffn1_w_up[layer], ffn1_w_down[layer])
        n_ssd_in = SSD_WIDTH + SSD_CONV_DIM + 2 * SSD_HEADS
        wl = w_in[layer]
        win = jnp.concatenate(
            [wl[:, :n_ssd_in], jnp.zeros((d, DT_PAD - 2 * SSD_HEADS), wl.dtype), wl[:, n_ssd_in:]],
            axis=1).astype(BF16)
        x1, z, xbc, dt, q, k, v = _ffn_inproj_call(
            xf, row(ffn1_norm_g[layer]), wg1, wu1, wd1, row(mix_norm_g[layer]), win,
            cos_t, sa_t, sb_t, s, tm)
        cw = jnp.concatenate([conv_w[layer].astype(F32),
                              jnp.zeros((SUBLANES - SSD_CONV, SSD_CONV_DIM), F32)], axis=0)
        dtbias = _pad_lanes(jnp.concatenate([dt_bias_fwd[layer], dt_bias_bwd[layer]]), 0.0)
        alog = _pad_lanes(jnp.concatenate([A_log_fwd[layer], A_log_bwd[layer]]), -1e4)
        dexp = jnp.repeat(D_skip[layer].astype(F32), SSD_HEAD_DIM)[None, :]
        yf, yb = _ssd_call(xbc.reshape(b, s, SSD_CONV_DIM), dt.reshape(b, s, DT_PAD),
                           cw, row(conv_b[layer]), dtbias, alog, dexp, rc)
        lamv = jnp.stack([lambda_q1[layer], lambda_k1[layer],
                          lambda_q2[layer], lambda_k2[layer]]).astype(F32)
        yd = _diffattn_call(q.reshape(b, s, DIFF_WIDTH), k.reshape(b, s, DIFF_WIDTH),
                            v.reshape(b, s, DIFF_WIDTH), lamv, row(diff_subln_g[layer]),
                            lambda_init, tq)
        wg2, wu2, wd2 = _ff_chunks(ffn2_w_gate[layer], ffn2_w_up[layer], ffn2_w_down[layer])
        gfin = row(final_norm_g)
        xf = _tail_call(x1, yf.reshape(b * s, SSD_WIDTH), yb.reshape(b * s, SSD_WIDTH), z,
                        yd.reshape(b * s, DIFF_WIDTH), kt, vm,
                        row(ssd_norm_g[layer]), w_out[layer].astype(BF16), row(xattn_norm_g[layer]),
                        xattn_w_q[layer].astype(BF16), xattn_w_o[layer].astype(BF16),
                        row(ffn2_norm_g[layer]), wg2, wu2, wd2, gfin, s, tm)
    return xf.reshape(b, s, d)
```

```python
import functools
import math

import jax
import jax.numpy as jnp
from jax import lax
from jax.experimental import pallas as pl
from jax.experimental.pallas import tpu as pltpu

F32 = jnp.float32
BF16 = jnp.bfloat16

NORM_EPS = 1e-5
D_MODEL = 1024
D_FF = 2816
FF_CHUNK = 256
N_FF_CHUNKS = D_FF // FF_CHUNK
SSD_WIDTH = 512
SSD_HEADS = 8
SSD_HEAD_DIM = 64
SSD_GROUPS = 2
SSD_STATE = 128
SSD_CONV = 5
SSD_CHUNK = 128
SSD_CONV_DIM = 1024
DIFF_WIDTH = 512
DIFF_HEADS = 4
DIFF_HEAD_DIM = 64
DIFF_V_DIM = 128
ROPE_THETA = 500000.0
ROPE_DIM = 16
XATTN_HEADS = 4
XATTN_HEAD_DIM = 256
LANES = 128
SUBLANES = 8
BF16_ROWS = 2 * SUBLANES
DT_PAD = LANES
IN_Z, IN_XBC, IN_DT = 0, SSD_WIDTH, SSD_WIDTH + SSD_CONV_DIM
IN_K = IN_DT + DT_PAD
IN_TOTAL = IN_K + DIFF_WIDTH
VMEM_LIMIT = 56 * 1024 * 1024


def _dot(a, b):
    return jnp.dot(a, b, preferred_element_type=F32)


def _rms(xf, g):
    ms = jnp.sum(xf * xf, axis=-1, keepdims=True) * (1.0 / xf.shape[-1])
    return xf * lax.rsqrt(ms + NORM_EPS) * g


def _silu(x):
    return x * jax.nn.sigmoid(x)


def _swiglu_residual(x, g_ref, wg_ref, wu_ref, wd_ref, acc_ref):
    h = _rms(x, g_ref[...]).astype(BF16)
    acc_ref[...] = jnp.zeros_like(acc_ref)

    def body(c, carry):
        gate = _dot(h, wg_ref[c])
        up = _dot(h, wu_ref[c])
        act = (_silu(gate) * up).astype(BF16)
        acc_ref[...] += _dot(act, wd_ref[c])
        return carry

    lax.fori_loop(0, N_FF_CHUNKS, body, 0, unroll=True)
    return x + 0.5 * acc_ref[...]


def _const_spec(shape):
    nd = len(shape)
    return pl.BlockSpec(shape, lambda *_: (0,) * nd, pipeline_mode=pl.Buffered(1))


def _kv_kernel(mem_ref, g_ref, wkt_ref, wv_ref, kt_ref, v_ref):
    m = _rms(mem_ref[...], g_ref[...]).astype(BF16)
    kt = lax.dot_general(wkt_ref[...], m, (((1,), (1,)), ((), ())),
                         preferred_element_type=F32)
    kt_ref[...] = kt.astype(BF16)
    v_ref[...] = _dot(m, wv_ref[...]).astype(BF16)


def _kv_call(mem, g, wkt, wv):
    b, m, d = mem.shape
    return pl.pallas_call(
        _kv_kernel,
        grid=(b,),
        in_specs=[pl.BlockSpec((None, m, d), lambda i: (i, 0, 0)),
                  _const_spec((1, d)), _const_spec((d, d)), _const_spec((d, d))],
        out_specs=[pl.BlockSpec((None, d, m), lambda i: (i, 0, 0)),
                   pl.BlockSpec((None, m, d), lambda i: (i, 0, 0))],
        out_shape=[jax.ShapeDtypeStruct((b, d, m), BF16), jax.ShapeDtypeStruct((b, m, d), BF16)],
        compiler_params=pltpu.CompilerParams(dimension_semantics=("arbitrary",),
                                             vmem_limit_bytes=VMEM_LIMIT),
        name="kv",
    )(mem, g, wkt, wv)


def _rope(t, cos, sa, sb):
    return t * cos + pltpu.roll(t, 8, axis=1) * sa + pltpu.roll(t, LANES - 8, axis=1) * sb


def _dot_nt(a, b):
    return lax.dot_general(a, b, (((1,), (1,)), ((), ())), preferred_element_type=F32)


def _rope_rows(tt, cos, sin):
    half = ROPE_DIM // 2
    pieces = []
    for base in range(0, DIFF_WIDTH, DIFF_HEAD_DIM):
        t1 = tt[base:base + half]
        t2 = tt[base + half:base + ROPE_DIM]
        pieces += [t1 * cos - t2 * sin, t2 * cos + t1 * sin, tt[base + ROPE_DIM:base + DIFF_HEAD_DIM]]
    return jnp.concatenate(pieces, axis=0)


def _ffn_inproj_kernel(x_ref, g1_ref, wg_ref, wu_ref, wd_ref, gmix_ref, win_ref, wqt_ref, wvt_ref,
                       cos_ref, sa_ref, sb_ref, cosr_ref, sinr_ref,
                       x1_ref, z_ref, xbc_ref, dt_ref, qt_ref, k_ref, vt_ref, acc_ref):
    x1 = _swiglu_residual(x_ref[...], g1_ref, wg_ref, wu_ref, wd_ref, acc_ref)
    x1_ref[...] = x1
    h = _rms(x1, gmix_ref[...]).astype(BF16)
    z_ref[...] = _dot(h, win_ref[:, IN_Z:IN_XBC])
    xbc_ref[...] = _dot(h, win_ref[:, IN_XBC:IN_DT])
    dt_ref[...] = _dot(h, win_ref[:, IN_DT:IN_K])
    vt_ref[...] = _dot_nt(wvt_ref[...], h).astype(BF16)
    qt = _dot_nt(wqt_ref[...], h) * (DIFF_HEAD_DIM ** -0.5 * math.log2(math.e))
    qt_ref[...] = _rope_rows(qt, cosr_ref[...], sinr_ref[...]).astype(BF16)
    cos, sa, sb = cos_ref[...], sa_ref[...], sb_ref[...]
    k = _dot(h, win_ref[:, IN_K:IN_TOTAL])
    for s in range(DIFF_WIDTH // LANES):
        sl = slice(s * LANES, (s + 1) * LANES)
        k_ref[:, sl] = _rope(k[:, sl], cos, sa, sb).astype(BF16)


def _ffn_inproj_call(xf, g1, wg, wu, wd, gmix, win, wqt, wvt, cos_t, sa_t, sb_t, cos_r, sin_r, seq, tm):
    t, d = xf.shape
    nseq = seq // tm
    row = lambda w: pl.BlockSpec((tm, w), lambda i: (i, 0))
    colb = pl.BlockSpec((DIFF_WIDTH, tm), lambda i: (0, i))
    tab = pl.BlockSpec((tm, LANES), lambda i: (i % nseq, 0))
    tabr = pl.BlockSpec((ROPE_DIM // 2, tm), lambda i: (0, i % nseq))
    return pl.pallas_call(
        _ffn_inproj_kernel,
        grid=(t // tm,),
        in_specs=[row(d), _const_spec((1, d)),
                  _const_spec(wg.shape), _const_spec(wu.shape), _const_spec(wd.shape),
                  _const_spec((1, d)), _const_spec(win.shape), _const_spec(wqt.shape),
                  _const_spec(wvt.shape), tab, tab, tab, tabr, tabr],
        out_specs=[row(d), row(SSD_WIDTH), row(SSD_CONV_DIM), row(DT_PAD),
                   colb, row(DIFF_WIDTH), colb],
        out_shape=[jax.ShapeDtypeStruct((t, d), F32),
                   jax.ShapeDtypeStruct((t, SSD_WIDTH), F32),
                   jax.ShapeDtypeStruct((t, SSD_CONV_DIM), F32),
                   jax.ShapeDtypeStruct((t, DT_PAD), F32),
                   jax.ShapeDtypeStruct((DIFF_WIDTH, t), BF16),
                   jax.ShapeDtypeStruct((t, DIFF_WIDTH), BF16),
                   jax.ShapeDtypeStruct((DIFF_WIDTH, t), BF16)],
        scratch_shapes=[pltpu.VMEM((tm, d), F32)],
        compiler_params=pltpu.CompilerParams(dimension_semantics=("arbitrary",),
                                             vmem_limit_bytes=VMEM_LIMIT),
        name="ffn_inproj",
    )(xf, g1, wg, wu, wd, gmix, win, wqt, wvt, cos_t, sa_t, sb_t, cos_r, sin_r)


def _split_dot(tri, a):
    a0 = a.astype(BF16)
    r1 = a - a0.astype(F32)
    a1 = r1.astype(BF16)
    a2 = (r1 - a1.astype(F32)).astype(BF16)
    return _dot(tri, a0) + _dot(tri, a1) + _dot(tri, a2)


def _pair_expand(tile, c0):
    rows = tile.shape[0]
    lane = lax.broadcasted_iota(jnp.int32, (rows, LANES), 1)
    lo = jnp.broadcast_to(tile[:, c0:c0 + 1], (rows, LANES))
    hi = jnp.broadcast_to(tile[:, c0 + 1:c0 + 2], (rows, LANES))
    return jnp.where(lane < SSD_HEAD_DIM, lo, hi)


def _ssd_chunk(xc, dtraw, dtbias, a_neg, state_ref, d, with_skip, dexp):
    L = SSD_CHUNK
    row = lax.broadcasted_iota(jnp.int32, (L, L), 0)
    col = lax.broadcasted_iota(jnp.int32, (L, L), 1)
    xs = xc[:, :SSD_WIDTH]
    v = dtraw + dtbias
    dt = jnp.maximum(v, 0.0) + jnp.log1p(jnp.exp(-jnp.abs(v)))
    a = dt * a_neg
    if d == 0:
        valid = row >= col
    else:
        valid = row <= col
    acum = _split_dot(jnp.where(valid, 1.0, 0.0).astype(BF16), a)
    tot = acum[L - 1:L, :] if d == 0 else acum[0:1, :]
    eac = jnp.exp(acum)
    wl = dt * jnp.exp(tot - acum)
    cdec = jnp.exp(tot)
    acum_t = acum.T
    dt_t = dt.T
    lane = lax.broadcasted_iota(jnp.int32, (L, LANES), 1)
    lane2 = lax.broadcasted_iota(jnp.int32, (2 * L, LANES), 1)
    row2 = lax.broadcasted_iota(jnp.int32, (2 * L, LANES), 0)
    bd_mask = jnp.right_shift(row2, 7) == jnp.right_shift(lane2, 6)
    ys = []
    for g in range(SSD_GROUPS):
        bg = xc[:, SSD_WIDTH + g * SSD_STATE:SSD_WIDTH + (g + 1) * SSD_STATE]
        cg = xc[:, SSD_WIDTH + (SSD_GROUPS + g) * SSD_STATE:SSD_WIDTH + (SSD_GROUPS + g + 1) * SSD_STATE]
        bt = bg.T.astype(BF16)
        cb16 = cg.astype(BF16)
        cb = _dot(cb16, bt)
        prev = state_ref[d, g]
        yoff = _dot(cb16, prev.astype(BF16))
        w_parts, cd_parts = [], []
        for p in range(2):
            h0 = 4 * g + 2 * p
            c0 = SSD_HEADS * d + h0
            ms = []
            for c in (c0, c0 + 1):
                diff = acum[:, c:c + 1] - acum_t[c:c + 1, :]
                dec = jnp.exp(jnp.where(valid, diff, -jnp.inf))
                ms.append((cb * dec * dt_t[c:c + 1, :]).astype(BF16))
            lhs = jnp.concatenate(ms, axis=1)
            xp = xs[:, h0 * SSD_HEAD_DIM:(h0 + 2) * SSD_HEAD_DIM]
            xp2 = jnp.concatenate([xp, xp], axis=0)
            rhs = jnp.where(bd_mask, xp2, 0.0).astype(BF16)
            ydiag = _dot(lhs, rhs)
            y = ydiag + yoff[:, p * LANES:(p + 1) * LANES] * _pair_expand(eac, c0)
            if with_skip:
                y = y + dexp[:, h0 * SSD_HEAD_DIM:(h0 + 2) * SSD_HEAD_DIM] * xp
            ys.append(y)
            w_parts.append((xp * _pair_expand(wl, c0)).astype(BF16))
            cd_parts.append(_pair_expand(cdec, c0))
        w = jnp.concatenate(w_parts, axis=1)
        cd = jnp.concatenate(cd_parts, axis=1)
        state_ref[d, g] = prev * cd + _dot(bt, w)
    return jnp.concatenate(ys, axis=1)


def _ssd_kernel(xf_ref, xfp_ref, xfn_ref, xb_ref, xbp_ref, xbn_ref, dtf_ref, dtb_ref,
                cw_ref, cbias_ref, dtbias_ref, alog_ref, dexp_ref,
                yf_ref, yb_ref, state_ref, pad_ref, *, rc):
    j = pl.program_id(1)
    nb = pl.num_programs(1)
    rows = rc * SSD_CHUNK

    @pl.when(j == 0)
    def _():
        state_ref[...] = jnp.zeros_like(state_ref)

    a_neg = -jnp.exp(alog_ref[...])
    dtbias = dtbias_ref[...]
    dexp = dexp_ref[...]
    cw = cw_ref[...]
    cbias = cbias_ref[...]

    def conv(x_ref, prev_ref, next_ref, blk):
        pad_ref[0:SUBLANES, :] = jnp.where(blk > 0, prev_ref[...], 0.0)
        pad_ref[SUBLANES:SUBLANES + rows, :] = x_ref[...]
        pad_ref[SUBLANES + rows:2 * SUBLANES + rows, :] = jnp.where(blk < nb - 1, next_ref[...], 0.0)

    def conv_chunk(c):
        base = SUBLANES + c * SSD_CHUNK - SSD_CONV // 2
        acc = cbias
        for k in range(SSD_CONV):
            acc = acc + pad_ref[base + k:base + k + SSD_CHUNK, :] * cw[k:k + 1, :]
        return _silu(acc)

    conv(xf_ref, xfp_ref, xfn_ref, j)
    for c in range(rc):
        sl = slice(c * SSD_CHUNK, (c + 1) * SSD_CHUNK)
        yf_ref[sl, :] = _ssd_chunk(conv_chunk(c), dtf_ref[sl, :], dtbias, a_neg, state_ref, 0, True, dexp)
    conv(xb_ref, xbp_ref, xbn_ref, nb - 1 - j)
    for c in reversed(range(rc)):
        sl = slice(c * SSD_CHUNK, (c + 1) * SSD_CHUNK)
        yb_ref[sl, :] = _ssd_chunk(conv_chunk(c), dtb_ref[sl, :], dtbias, a_neg, state_ref, 1, False, dexp)


def _ssd_call(xbc, dt, cw, cbias, dtbias, alog, dexp, rc):
    b, s, cdim = xbc.shape
    rows = rc * SSD_CHUNK
    nb = s // rows
    hb = rows // SUBLANES
    nh = s // SUBLANES
    main = lambda w, rev: pl.BlockSpec(
        (None, rows, w), (lambda i, j: (i, nb - 1 - j, 0)) if rev else (lambda i, j: (i, j, 0)))

    def halo(rev, nxt):
        def idx(i, j):
            blk = nb - 1 - j if rev else j
            h = (blk + 1) * hb if nxt else blk * hb - 1
            return (i, jnp.clip(h, 0, nh - 1), 0)
        return pl.BlockSpec((None, SUBLANES, cdim), idx)

    return pl.pallas_call(
        functools.partial(_ssd_kernel, rc=rc),
        grid=(b, nb),
        in_specs=[main(cdim, False), halo(False, False), halo(False, True),
                  main(cdim, True), halo(True, False), halo(True, True),
                  main(DT_PAD, False), main(DT_PAD, True),
                  _const_spec(cw.shape), _const_spec(cbias.shape), _const_spec(dtbias.shape),
                  _const_spec(alog.shape), _const_spec(dexp.shape)],
        out_specs=[main(SSD_WIDTH, False), main(SSD_WIDTH, True)],
        out_shape=[jax.ShapeDtypeStruct((b, s, SSD_WIDTH), F32),
                   jax.ShapeDtypeStruct((b, s, SSD_WIDTH), F32)],
        scratch_shapes=[pltpu.VMEM((2, SSD_GROUPS, SSD_STATE, 4 * SSD_HEAD_DIM), F32),
                        pltpu.VMEM((rows + 2 * SUBLANES, cdim), F32)],
        compiler_params=pltpu.CompilerParams(dimension_semantics=("arbitrary", "arbitrary"),
                                             vmem_limit_bytes=VMEM_LIMIT),
        name="ssd",
    )(xbc, xbc, xbc, xbc, xbc, xbc, dt, dt, cw, cbias, dtbias, alog, dexp)


def _diffattn_kernel(qt_ref, k_ref, vt_ref, lam_ref, g_ref, o_ref, s_ref, a_ref, m_ref,
                     *, lambda_init, tq):
    seq = k_ref.shape[0]
    lamv = lam_ref[...]
    l1 = jnp.sum(lamv[0:1] * lamv[1:2], axis=-1, keepdims=True)
    l2 = jnp.sum(lamv[2:3] * lamv[3:4], axis=-1, keepdims=True)
    lam = jnp.exp(l1) - jnp.exp(l2) + lambda_init
    kf = k_ref[...].astype(F32)
    lane = lax.broadcasted_iota(jnp.int32, kf.shape, 1)
    k_half = [jnp.where(lane < DIFF_HEAD_DIM, kf, 0.0).astype(BF16),
              jnp.where(lane >= DIFF_HEAD_DIM, kf, 0.0).astype(BF16)]
    vt_aug = jnp.concatenate([vt_ref[...], jnp.ones((BF16_ROWS, seq), BF16)], axis=0)
    gcol = g_ref[...] * (1.0 - lambda_init)
    nblk = seq // tq

    def scores(i, slot):
        qt = qt_ref[:, pl.ds(pl.multiple_of(i * tq, tq), tq)]
        for t in range(2):
            s = _dot(k_half[t], qt)
            s_ref[slot, t] = s
            m_ref[slot, t] = jnp.max(s, axis=0, keepdims=True)

    def softmax(slot):
        for t in range(2):
            a_ref[slot, t] = jnp.exp2(s_ref[slot, t] - m_ref[slot, t]).astype(BF16)

    def values(i, slot):
        halves = []
        for t in range(2):
            oa = _dot(vt_aug, a_ref[slot, t])
            halves.append(oa[:DIFF_V_DIM] * (1.0 / oa[DIFF_V_DIM:DIFF_V_DIM + 1]))
        o = halves[0] - lam * halves[1]
        ms = jnp.sum(o * o, axis=0, keepdims=True) * (1.0 / DIFF_V_DIM)
        y = o * lax.rsqrt(ms + NORM_EPS) * gcol
        o_ref[pl.ds(pl.multiple_of(i * tq, tq), tq), :] = y.T.astype(o_ref.dtype)

    scores(0, 0)
    scores(1, 1)
    softmax(0)

    def pair(j, carry):
        i = 2 * j + 1
        scores(i + 1, 0)
        values(i - 1, 0)
        softmax(1)
        scores(i + 2, 1)
        values(i, 1)
        softmax(0)
        return carry

    lax.fori_loop(0, (nblk - 2) // 2, pair, 0)
    values(nblk - 2, 0)
    softmax(1)
    values(nblk - 1, 1)


def _diffattn_call(qt, k, vt, lamv, gcol, lambda_init, seq, tq):
    t = k.shape[0]
    assert (seq // tq) % 2 == 0 and seq // tq >= 2
    col = pl.BlockSpec((DIFF_V_DIM, seq), lambda i, h: (h, i))
    row = pl.BlockSpec((seq, DIFF_V_DIM), lambda i, h: (i, h))
    return pl.pallas_call(
        functools.partial(_diffattn_kernel, lambda_init=lambda_init, tq=tq),
        grid=(t // seq, DIFF_HEADS),
        in_specs=[col, row, col, _const_spec(lamv.shape), _const_spec(gcol.shape)],
        out_specs=row,
        out_shape=jax.ShapeDtypeStruct((t, DIFF_WIDTH), BF16),
        scratch_shapes=[pltpu.VMEM((2, 2, seq, tq), F32),
                        pltpu.VMEM((2, 2, seq, tq), BF16),
                        pltpu.VMEM((2, 2, 1, tq), F32)],
        compiler_params=pltpu.CompilerParams(dimension_semantics=("arbitrary", "arbitrary"),
                                             vmem_limit_bytes=VMEM_LIMIT),
        name="diffattn",
    )(qt, k, vt, lamv, gcol)


def _tail_kernel(x1_ref, yf_ref, yb_ref, z_ref, yd_ref, kt_ref, vm_ref,
                 gssd_ref, wout_ref, gx_ref, wq_ref, wo_ref,
                 g2_ref, wg_ref, wu_ref, wd_ref, gfin_ref,
                 o_ref, acc_ref):
    y = (yf_ref[...] + yb_ref[...]) * _silu(z_ref[...])
    gw = SSD_WIDTH // SSD_GROUPS
    gssd = gssd_ref[...]
    yn = jnp.concatenate(
        [_rms(y[:, g * gw:(g + 1) * gw], gssd[:, g * gw:(g + 1) * gw]) for g in range(SSD_GROUPS)],
        axis=1).astype(BF16)
    x2 = x1_ref[...] + _dot(yn, wout_ref[0:SSD_WIDTH, :]) + _dot(yd_ref[...], wout_ref[SSD_WIDTH:, :])
    h = _rms(x2, gx_ref[...]).astype(BF16)
    qx = (_dot(h, wq_ref[...]) * (XATTN_HEAD_DIM ** -0.5)).astype(BF16)
    heads = []
    for hd in range(XATTN_HEADS):
        sl = slice(hd * XATTN_HEAD_DIM, (hd + 1) * XATTN_HEAD_DIM)
        s = _dot(qx[:, sl], kt_ref[sl, :])
        p = jnp.exp(s - jnp.max(s, axis=-1, keepdims=True))
        p = p * (1.0 / jnp.sum(p, axis=-1, keepdims=True))
        heads.append(_dot(p.astype(BF16), vm_ref[:, sl]).astype(BF16))
    x3 = x2 + _dot(jnp.concatenate(heads, axis=1), wo_ref[...])
    x4 = _swiglu_residual(x3, g2_ref, wg_ref, wu_ref, wd_ref, acc_ref)
    o_ref[...] = _rms(x4, gfin_ref[...])


def _tail_call(x1, yf, yb, z, yd, kt, vm, gssd, wout, gx, wq, wo, g2, wg, wu, wd, gfin, seq, tm):
    t, d = x1.shape
    nseq = seq // tm
    m = vm.shape[1]
    row = lambda w: pl.BlockSpec((tm, w), lambda i: (i, 0))
    return pl.pallas_call(
        _tail_kernel,
        grid=(t // tm,),
        in_specs=[row(d), row(SSD_WIDTH), row(SSD_WIDTH), row(SSD_WIDTH), row(DIFF_WIDTH),
                  pl.BlockSpec((None, d, m), lambda i: (i // nseq, 0, 0)),
                  pl.BlockSpec((None, m, d), lambda i: (i // nseq, 0, 0)),
                  _const_spec(gssd.shape), _const_spec(wout.shape), _const_spec(gx.shape),
                  _const_spec(wq.shape), _const_spec(wo.shape), _const_spec(g2.shape),
                  _const_spec(wg.shape), _const_spec(wu.shape), _const_spec(wd.shape),
                  _const_spec(gfin.shape)],
        out_specs=row(d),
        out_shape=jax.ShapeDtypeStruct((t, d), F32),
        scratch_shapes=[pltpu.VMEM((tm, d), F32)],
        compiler_params=pltpu.CompilerParams(dimension_semantics=("arbitrary",),
                                             vmem_limit_bytes=VMEM_LIMIT),
        name="tail",
    )(x1, yf, yb, z, yd, kt, vm, gssd, wout, gx, wq, wo, g2, wg, wu, wd, gfin)


def _rope_lane_tables(seq):
    pos = jnp.arange(seq, dtype=F32)
    inv_freq = 1.0 / (ROPE_THETA ** (jnp.arange(0, ROPE_DIM, 2, dtype=F32) / ROPE_DIM))
    ang = pos[:, None] * inv_freq[None, :]
    cos, sin = jnp.cos(ang), jnp.sin(ang)
    half = ROPE_DIM // 2
    pad = jnp.zeros((seq, DIFF_HEAD_DIM - ROPE_DIM), F32)
    zeros = jnp.zeros((seq, half), F32)
    cos64 = jnp.concatenate([cos, cos, pad + 1.0], axis=1)
    sa64 = jnp.concatenate([zeros, sin, pad], axis=1)
    sb64 = jnp.concatenate([-sin, zeros, pad], axis=1)
    tile = lambda a: jnp.concatenate([a, a], axis=1)
    return tile(cos64), tile(sa64), tile(sb64), cos.T, sin.T


def _ff_chunks(w_gate, w_up, w_down):
    d = w_gate.shape[0]
    wg = w_gate.astype(BF16).reshape(d, N_FF_CHUNKS, FF_CHUNK).transpose(1, 0, 2)
    wu = w_up.astype(BF16).reshape(d, N_FF_CHUNKS, FF_CHUNK).transpose(1, 0, 2)
    wd = w_down.astype(BF16).reshape(N_FF_CHUNKS, FF_CHUNK, d)
    return wg, wu, wd


def _pad_lanes(v, fill):
    return jnp.concatenate([v.astype(F32), jnp.full((LANES - v.shape[0],), fill, F32)])[None, :]


def kernel(x, mem, ffn1_norm_g, ffn1_w_gate, ffn1_w_up, ffn1_w_down, mix_norm_g, w_in, conv_w, conv_b, dt_bias_fwd, dt_bias_bwd, A_log_fwd, A_log_bwd, D_skip, ssd_norm_g, lambda_q1, lambda_k1, lambda_q2, lambda_k2, diff_subln_g, w_out, xattn_norm_g, mem_norm_g, xattn_w_q, xattn_w_kv, xattn_w_o, ffn2_norm_g, ffn2_w_gate, ffn2_w_up, ffn2_w_down, final_norm_g):
    b, s, d = x.shape
    depth = w_in.shape[0]
    assert depth == 1, "single-layer block only"
    tm = 512
    tq = 256
    rc = 1
    cos_t, sa_t, sb_t, cos_r, sin_r = _rope_lane_tables(s)
    xf = x.reshape(b * s, d)
    row = lambda v: v.astype(F32)[None, :]
    for layer in range(depth):
        lambda_init = 0.8 - 0.6 * math.exp(-0.3 * layer)
        w_kv = xattn_w_kv[layer]
        kt, vm = _kv_call(mem, row(mem_norm_g[layer]),
                          w_kv[:, :d].T.astype(BF16), w_kv[:, d:].astype(BF16))
        wg1, wu1, wd1 = _ff_chunks(ffn1_w_gate[layer], ffn1_w_up[layer], ffn1_w_down[layer])
        n_ssd_in = SSD_WIDTH + SSD_CONV_DIM + 2 * SSD_HEADS
        wl = w_in[layer]
        w_q = wl[:, n_ssd_in:n_ssd_in + DIFF_WIDTH]
        w_k = wl[:, n_ssd_in + DIFF_WIDTH:n_ssd_in + 2 * DIFF_WIDTH]
        w_v = wl[:, n_ssd_in + 2 * DIFF_WIDTH:]
        win = jnp.concatenate(
            [wl[:, :n_ssd_in], jnp.zeros((d, DT_PAD - 2 * SSD_HEADS), wl.dtype), w_k],
            axis=1).astype(BF16)
        x1, z, xbc, dt, qt, k, vt = _ffn_inproj_call(
            xf, row(ffn1_norm_g[layer]), wg1, wu1, wd1, row(mix_norm_g[layer]), win,
            w_q.T.astype(BF16), w_v.T.astype(BF16), cos_t, sa_t, sb_t, cos_r, sin_r, s, tm)
        cw = jnp.concatenate([conv_w[layer].astype(F32),
                              jnp.zeros((SUBLANES - SSD_CONV, SSD_CONV_DIM), F32)], axis=0)
        dtbias = _pad_lanes(jnp.concatenate([dt_bias_fwd[layer], dt_bias_bwd[layer]]), 0.0)
        alog = _pad_lanes(jnp.concatenate([A_log_fwd[layer], A_log_bwd[layer]]), -1e4)
        dexp = jnp.repeat(D_skip[layer].astype(F32), SSD_HEAD_DIM)[None, :]
        yf, yb = _ssd_call(xbc.reshape(b, s, SSD_CONV_DIM), dt.reshape(b, s, DT_PAD),
                           cw, row(conv_b[layer]), dtbias, alog, dexp, rc)
        lamv = jnp.stack([lambda_q1[layer], lambda_k1[layer],
                          lambda_q2[layer], lambda_k2[layer]]).astype(F32)
        yd = _diffattn_call(qt, k, vt, lamv, diff_subln_g[layer].astype(F32)[:, None],
                            lambda_init, s, tq)
        wg2, wu2, wd2 = _ff_chunks(ffn2_w_gate[layer], ffn2_w_up[layer], ffn2_w_down[layer])
        gfin = row(final_norm_g)
        xf = _tail_call(x1, yf.reshape(b * s, SSD_WIDTH), yb.reshape(b * s, SSD_WIDTH), z,
                        yd, kt, vm,
                        row(ssd_norm_g[layer]), w_out[layer].astype(BF16), row(xattn_norm_g[layer]),
                        xattn_w_q[layer].astype(BF16), xattn_w_o[layer].astype(BF16),
                        row(ffn2_norm_g[layer]), wg2, wu2, wd2, gfin, s, tm)
    return xf.reshape(b, s, d)
```

```python
import functools
import math

import jax
import jax.numpy as jnp
from jax import lax
from jax.experimental import pallas as pl
from jax.experimental.pallas import tpu as pltpu

F32 = jnp.float32
BF16 = jnp.bfloat16

NORM_EPS = 1e-5
D_MODEL = 1024
D_FF = 2816
FF_CHUNK = 256
N_FF_CHUNKS = D_FF // FF_CHUNK
SSD_WIDTH = 512
SSD_HEADS = 8
SSD_HEAD_DIM = 64
SSD_GROUPS = 2
SSD_STATE = 128
SSD_CONV = 5
SSD_CHUNK = 128
SSD_CONV_DIM = 1024
DIFF_WIDTH = 512
DIFF_HEADS = 4
DIFF_HEAD_DIM = 64
DIFF_V_DIM = 128
ROPE_THETA = 500000.0
ROPE_DIM = 16
XATTN_HEADS = 4
XATTN_HEAD_DIM = 256
LANES = 128
SUBLANES = 8
BF16_ROWS = 2 * SUBLANES
VT_ROWS = DIFF_V_DIM + BF16_ROWS
DT_PAD = LANES
IN_Z, IN_XBC, IN_DT = 0, SSD_WIDTH, SSD_WIDTH + SSD_CONV_DIM
IN_K = IN_DT + DT_PAD
IN_TOTAL = IN_K + DIFF_WIDTH
VMEM_LIMIT = 56 * 1024 * 1024


def _dot(a, b):
    return jnp.dot(a, b, preferred_element_type=F32)


def _rms(xf, g):
    ms = jnp.sum(xf * xf, axis=-1, keepdims=True) * (1.0 / xf.shape[-1])
    return xf * lax.rsqrt(ms + NORM_EPS) * g


def _silu(x):
    return x * jax.nn.sigmoid(x)


def _swiglu_residual(x, g_ref, wg_ref, wu_ref, wd_ref, acc_ref):
    h = _rms(x, g_ref[...]).astype(BF16)
    acc_ref[...] = jnp.zeros_like(acc_ref)

    def body(c, carry):
        gate = _dot(h, wg_ref[c])
        up = _dot(h, wu_ref[c])
        act = (_silu(gate) * up).astype(BF16)
        acc_ref[...] += _dot(act, wd_ref[c])
        return carry

    lax.fori_loop(0, N_FF_CHUNKS, body, 0, unroll=True)
    return x + 0.5 * acc_ref[...]


def _const_spec(shape):
    nd = len(shape)
    return pl.BlockSpec(shape, lambda *_: (0,) * nd, pipeline_mode=pl.Buffered(1))


def _kv_kernel(mem_ref, g_ref, wkt_ref, wv_ref, kt_ref, v_ref):
    m = _rms(mem_ref[...], g_ref[...]).astype(BF16)
    kt = lax.dot_general(wkt_ref[...], m, (((1,), (1,)), ((), ())),
                         preferred_element_type=F32)
    kt_ref[...] = kt.astype(BF16)
    v_ref[...] = _dot(m, wv_ref[...]).astype(BF16)


def _kv_call(mem, g, wkt, wv):
    b, m, d = mem.shape
    return pl.pallas_call(
        _kv_kernel,
        grid=(b,),
        in_specs=[pl.BlockSpec((None, m, d), lambda i: (i, 0, 0)),
                  _const_spec((1, d)), _const_spec((d, d)), _const_spec((d, d))],
        out_specs=[pl.BlockSpec((None, d, m), lambda i: (i, 0, 0)),
                   pl.BlockSpec((None, m, d), lambda i: (i, 0, 0))],
        out_shape=[jax.ShapeDtypeStruct((b, d, m), BF16), jax.ShapeDtypeStruct((b, m, d), BF16)],
        compiler_params=pltpu.CompilerParams(dimension_semantics=("arbitrary",),
                                             vmem_limit_bytes=VMEM_LIMIT),
        name="kv",
    )(mem, g, wkt, wv)


def _rope(t, cos, sa, sb):
    return t * cos + pltpu.roll(t, 8, axis=1) * sa + pltpu.roll(t, LANES - 8, axis=1) * sb


def _dot_nt(a, b):
    return lax.dot_general(a, b, (((1,), (1,)), ((), ())), preferred_element_type=F32)


def _rope_rows(tt, cos, sin):
    half = ROPE_DIM // 2
    pieces = []
    for base in range(0, DIFF_WIDTH, DIFF_HEAD_DIM):
        t1 = tt[base:base + half]
        t2 = tt[base + half:base + ROPE_DIM]
        pieces += [t1 * cos - t2 * sin, t2 * cos + t1 * sin, tt[base + ROPE_DIM:base + DIFF_HEAD_DIM]]
    return jnp.concatenate(pieces, axis=0)


def _ffn_inproj_kernel(x_ref, g1_ref, wg_ref, wu_ref, wd_ref, gmix_ref, win_ref, wqt_ref, wvt_ref,
                       cos_ref, sa_ref, sb_ref, cosr_ref, sinr_ref,
                       x1_ref, z_ref, xbc_ref, dt_ref, qt_ref, k_ref, vt_ref, acc_ref):
    x1 = _swiglu_residual(x_ref[...], g1_ref, wg_ref, wu_ref, wd_ref, acc_ref)
    x1_ref[...] = x1
    h = _rms(x1, gmix_ref[...]).astype(BF16)
    hp = _permute_chunks(_chunk_perm(False), h)
    z_ref[...] = _dot(hp, win_ref[:, IN_Z:IN_XBC])
    xbc_ref[...] = _dot(hp, win_ref[:, IN_XBC:IN_DT])
    dt_ref[...] = _dot(hp, win_ref[:, IN_DT:IN_K])
    vt = _dot_nt(wvt_ref[...], h).astype(BF16)
    ones = jnp.ones((BF16_ROWS, vt.shape[1]), BF16)
    for hd in range(DIFF_HEADS):
        vt_ref[hd * VT_ROWS:hd * VT_ROWS + DIFF_V_DIM, :] = vt[hd * DIFF_V_DIM:(hd + 1) * DIFF_V_DIM]
        vt_ref[hd * VT_ROWS + DIFF_V_DIM:(hd + 1) * VT_ROWS, :] = ones
    qt = _dot_nt(wqt_ref[...], h) * (DIFF_HEAD_DIM ** -0.5 * math.log2(math.e))
    qt_ref[...] = _rope_rows(qt, cosr_ref[...], sinr_ref[...]).astype(BF16)
    cos, sa, sb = cos_ref[...], sa_ref[...], sb_ref[...]
    k = _dot(h, win_ref[:, IN_K:IN_TOTAL])
    for s in range(DIFF_WIDTH // LANES):
        sl = slice(s * LANES, (s + 1) * LANES)
        k_ref[:, sl] = _rope(k[:, sl], cos, sa, sb).astype(BF16)


def _ffn_inproj_call(xf, g1, wg, wu, wd, gmix, win, wqt, wvt, cos_t, sa_t, sb_t, cos_r, sin_r, seq, tm):
    t, d = xf.shape
    nseq = seq // tm
    row = lambda w: pl.BlockSpec((tm, w), lambda i: (i, 0))
    colb = lambda rows: pl.BlockSpec((rows, tm), lambda i: (0, i))
    tab = pl.BlockSpec((tm, LANES), lambda i: (i % nseq, 0))
    tabr = pl.BlockSpec((ROPE_DIM // 2, tm), lambda i: (0, i % nseq))
    return pl.pallas_call(
        _ffn_inproj_kernel,
        grid=(t // tm,),
        in_specs=[row(d), _const_spec((1, d)),
                  _const_spec(wg.shape), _const_spec(wu.shape), _const_spec(wd.shape),
                  _const_spec((1, d)), _const_spec(win.shape), _const_spec(wqt.shape),
                  _const_spec(wvt.shape), tab, tab, tab, tabr, tabr],
        out_specs=[row(d), row(SSD_WIDTH), row(SSD_CONV_DIM), row(DT_PAD),
                   colb(DIFF_WIDTH), row(DIFF_WIDTH), colb(DIFF_HEADS * VT_ROWS)],
        out_shape=[jax.ShapeDtypeStruct((t, d), F32),
                   jax.ShapeDtypeStruct((t, SSD_WIDTH), F32),
                   jax.ShapeDtypeStruct((t, SSD_CONV_DIM), F32),
                   jax.ShapeDtypeStruct((t, DT_PAD), F32),
                   jax.ShapeDtypeStruct((DIFF_WIDTH, t), BF16),
                   jax.ShapeDtypeStruct((t, DIFF_WIDTH), BF16),
                   jax.ShapeDtypeStruct((DIFF_HEADS * VT_ROWS, t), BF16)],
        scratch_shapes=[pltpu.VMEM((tm, d), F32)],
        compiler_params=pltpu.CompilerParams(dimension_semantics=("arbitrary",),
                                             vmem_limit_bytes=VMEM_LIMIT),
        name="ffn_inproj",
    )(xf, g1, wg, wu, wd, gmix, win, wqt, wvt, cos_t, sa_t, sb_t, cos_r, sin_r)


def _split_dot(tri, a):
    a0 = a.astype(BF16)
    r1 = a - a0.astype(F32)
    a1 = r1.astype(BF16)
    a2 = (r1 - a1.astype(F32)).astype(BF16)
    return _dot(tri, a0) + _dot(tri, a1) + _dot(tri, a2)


def _pair_expand(tile, c0):
    rows = tile.shape[0]
    lane = lax.broadcasted_iota(jnp.int32, (rows, LANES), 1)
    lo = jnp.broadcast_to(tile[:, c0:c0 + 1], (rows, LANES))
    hi = jnp.broadcast_to(tile[:, c0 + 1:c0 + 2], (rows, LANES))
    return jnp.where(lane < SSD_HEAD_DIM, lo, hi)


def _chunk_time(r):
    nph = SSD_CHUNK // SUBLANES
    return jnp.right_shift(r, SUBLANES.bit_length() - 1) + (r & (SUBLANES - 1)) * nph


def _chunk_perm(transpose):
    row = lax.broadcasted_iota(jnp.int32, (SSD_CHUNK, SSD_CHUNK), 0)
    col = lax.broadcasted_iota(jnp.int32, (SSD_CHUNK, SSD_CHUNK), 1)
    hit = (row == _chunk_time(col)) if transpose else (col == _chunk_time(row))
    return jnp.where(hit, 1.0, 0.0).astype(BF16)


def _permute_chunks(perm, v):
    return jnp.concatenate(
        [_dot(perm, v[c:c + SSD_CHUNK]).astype(BF16) for c in range(0, v.shape[0], SSD_CHUNK)], axis=0)


def _ssd_chunk(xc, dtraw, dtbias, a_neg, state_ref, d, with_skip, dexp):
    L = SSD_CHUNK
    row = _chunk_time(lax.broadcasted_iota(jnp.int32, (L, L), 0))
    col = _chunk_time(lax.broadcasted_iota(jnp.int32, (L, L), 1))
    xs = xc[:, :SSD_WIDTH]
    v = dtraw + dtbias
    dt = jnp.maximum(v, 0.0) + jnp.log1p(jnp.exp(-jnp.abs(v)))
    a = dt * a_neg
    if d == 0:
        valid = row >= col
    else:
        valid = row <= col
    acum = _split_dot(jnp.where(valid, 1.0, 0.0).astype(BF16), a)
    tot = acum[L - 1:L, :] if d == 0 else acum[0:1, :]
    eac = jnp.exp(acum)
    wl = dt * jnp.exp(tot - acum)
    cdec = jnp.exp(tot)
    acum_t = acum.T
    dt_t = dt.T
    lane = lax.broadcasted_iota(jnp.int32, (L, LANES), 1)
    lane2 = lax.broadcasted_iota(jnp.int32, (2 * L, LANES), 1)
    row2 = lax.broadcasted_iota(jnp.int32, (2 * L, LANES), 0)
    bd_mask = (jnp.right_shift(row2, L.bit_length() - 1)
               == jnp.right_shift(lane2, SSD_HEAD_DIM.bit_length() - 1))
    ys = []
    for g in range(SSD_GROUPS):
        bg = xc[:, SSD_WIDTH + g * SSD_STATE:SSD_WIDTH + (g + 1) * SSD_STATE]
        cg = xc[:, SSD_WIDTH + (SSD_GROUPS + g) * SSD_STATE:SSD_WIDTH + (SSD_GROUPS + g + 1) * SSD_STATE]
        bt = bg.T.astype(BF16)
        cb16 = cg.astype(BF16)
        cb = _dot(cb16, bt)
        prev = state_ref[d, g]
        yoff = _dot(cb16, prev.astype(BF16))
        w_parts, cd_parts = [], []
        for p in range(2):
            h0 = 4 * g + 2 * p
            c0 = SSD_HEADS * d + h0
            ms = []
            for c in (c0, c0 + 1):
                diff = acum[:, c:c + 1] - acum_t[c:c + 1, :]
                dec = jnp.exp(jnp.where(valid, diff, -jnp.inf))
                ms.append((cb * dec * dt_t[c:c + 1, :]).astype(BF16))
            lhs = jnp.concatenate(ms, axis=1)
            xp = xs[:, h0 * SSD_HEAD_DIM:(h0 + 2) * SSD_HEAD_DIM]
            xp2 = jnp.concatenate([xp, xp], axis=0)
            rhs = jnp.where(bd_mask, xp2, 0.0).astype(BF16)
            ydiag = _dot(lhs, rhs)
            y = ydiag + yoff[:, p * LANES:(p + 1) * LANES] * _pair_expand(eac, c0)
            if with_skip:
                y = y + dexp[:, h0 * SSD_HEAD_DIM:(h0 + 2) * SSD_HEAD_DIM] * xp
            ys.append(y)
            w_parts.append((xp * _pair_expand(wl, c0)).astype(BF16))
            cd_parts.append(_pair_expand(cdec, c0))
        w = jnp.concatenate(w_parts, axis=1)
        cd = jnp.concatenate(cd_parts, axis=1)
        state_ref[d, g] = prev * cd + _dot(bt, w)
    return jnp.concatenate(ys, axis=1)


def _ssd_kernel(xf_ref, xfp_ref, xfn_ref, xb_ref, xbp_ref, xbn_ref, dtf_ref, dtb_ref,
                cw_ref, cbias_ref, dtbias_ref, alog_ref, dexp_ref,
                yf_ref, yb_ref, state_ref, *, rc):
    j = pl.program_id(1)
    nb = pl.num_programs(1)
    nph = SSD_CHUNK // SUBLANES
    pad = SSD_CONV // 2
    halo = pad * SUBLANES

    @pl.when(j == 0)
    def _():
        state_ref[...] = jnp.zeros_like(state_ref)

    a_neg = -jnp.exp(alog_ref[...])
    dtbias = dtbias_ref[...]
    dexp = dexp_ref[...]
    cdim = cw_ref.shape[1]
    taps = [jnp.broadcast_to(cw_ref[k:k + 1, :], (SUBLANES, cdim)) for k in range(SSD_CONV)]
    bias = jnp.broadcast_to(cbias_ref[...], (SUBLANES, cdim))
    sub = lax.broadcasted_iota(jnp.int32, (SUBLANES, cdim), 0)

    def conv_chunk(x_ref, prev_ref, next_ref, blk, c):
        lo = c * SSD_CHUNK
        slab = lambda p: x_ref[lo + p * SUBLANES:lo + (p + 1) * SUBLANES, :]
        if c > 0:
            before = x_ref[lo - halo:lo, :]
        else:
            before = jnp.where(blk > 0, prev_ref[...], 0.0)
        if c < rc - 1:
            after = x_ref[lo + SSD_CHUNK:lo + SSD_CHUNK + halo, :]
        else:
            after = jnp.where(blk < nb - 1, next_ref[...], 0.0)
        window = [None] * (nph + 2 * pad)
        for p in range(nph):
            window[p + pad] = slab(p)
        for q in range(pad):
            src = nph - pad + q
            prev_slab = before[q * SUBLANES:(q + 1) * SUBLANES, :]
            window[q] = jnp.where(sub == 0, pltpu.roll(prev_slab, 1, axis=0), pltpu.roll(slab(src), 1, axis=0))
            next_slab = after[q * SUBLANES:(q + 1) * SUBLANES, :]
            window[nph + pad + q] = jnp.where(sub == SUBLANES - 1,
                                              pltpu.roll(next_slab, SUBLANES - 1, axis=0),
                                              pltpu.roll(slab(q), SUBLANES - 1, axis=0))
        out = []
        for p in range(nph):
            acc = bias
            for k in range(SSD_CONV):
                acc = acc + window[p + k] * taps[k]
            out.append(_silu(acc))
        return jnp.concatenate(out, axis=0)

    for c in range(rc):
        cb_ = rc - 1 - c
        sf = slice(c * SSD_CHUNK, (c + 1) * SSD_CHUNK)
        sb = slice(cb_ * SSD_CHUNK, (cb_ + 1) * SSD_CHUNK)
        yf_ref[sf, :] = _ssd_chunk(conv_chunk(xf_ref, xfp_ref, xfn_ref, j, c), dtf_ref[sf, :],
                                   dtbias, a_neg, state_ref, 0, True, dexp)
        yb_ref[sb, :] = _ssd_chunk(conv_chunk(xb_ref, xbp_ref, xbn_ref, nb - 1 - j, cb_), dtb_ref[sb, :],
                                   dtbias, a_neg, state_ref, 1, False, dexp)


def _ssd_call(xbc, dt, cw, cbias, dtbias, alog, dexp, rc):
    b, s, cdim = xbc.shape
    rows = rc * SSD_CHUNK
    nb = s // rows
    halo_rows = (SSD_CONV // 2) * SUBLANES
    hb = rows // halo_rows
    nh = s // halo_rows
    main = lambda w, rev: pl.BlockSpec(
        (None, rows, w), (lambda i, j: (i, nb - 1 - j, 0)) if rev else (lambda i, j: (i, j, 0)))

    def halo(rev, nxt):
        def idx(i, j):
            blk = nb - 1 - j if rev else j
            h = (blk + 1) * hb if nxt else blk * hb - 1
            return (i, jnp.clip(h, 0, nh - 1), 0)
        return pl.BlockSpec((None, halo_rows, cdim), idx)

    return pl.pallas_call(
        functools.partial(_ssd_kernel, rc=rc),
        grid=(b, nb),
        in_specs=[main(cdim, False), halo(False, False), halo(False, True),
                  main(cdim, True), halo(True, False), halo(True, True),
                  main(DT_PAD, False), main(DT_PAD, True),
                  _const_spec(cw.shape), _const_spec(cbias.shape), _const_spec(dtbias.shape),
                  _const_spec(alog.shape), _const_spec(dexp.shape)],
        out_specs=[main(SSD_WIDTH, False), main(SSD_WIDTH, True)],
        out_shape=[jax.ShapeDtypeStruct((b, s, SSD_WIDTH), F32),
                   jax.ShapeDtypeStruct((b, s, SSD_WIDTH), F32)],
        scratch_shapes=[pltpu.VMEM((2, SSD_GROUPS, SSD_STATE, 4 * SSD_HEAD_DIM), F32)],
        compiler_params=pltpu.CompilerParams(dimension_semantics=("arbitrary", "arbitrary"),
                                             vmem_limit_bytes=VMEM_LIMIT),
        name="ssd",
    )(xbc, xbc, xbc, xbc, xbc, xbc, dt, dt, cw, cbias, dtbias, alog, dexp)


def _diffattn_kernel(qt_ref, k_ref, vt_ref, lam_ref, g_ref, o_ref, s_ref, a_ref, m_ref,
                     *, lambda_init, tq, hps):
    seq = k_ref.shape[0]
    lamv = lam_ref[...]
    l1 = jnp.sum(lamv[0:1] * lamv[1:2], axis=-1, keepdims=True)
    l2 = jnp.sum(lamv[2:3] * lamv[3:4], axis=-1, keepdims=True)
    lam = jnp.exp(l1) - jnp.exp(l2) + lambda_init
    gcol = g_ref[...] * (1.0 - lambda_init)
    nq = seq // tq
    nblk = hps * nq
    zero_half = jnp.zeros((DIFF_HEAD_DIM, tq), BF16)

    def locate(i):
        head = i // nq
        return head, pl.multiple_of((i - head * nq) * tq, tq)

    def scores(i, slot):
        head, off = locate(i)
        hoff = pl.multiple_of(head * DIFF_V_DIM, DIFF_V_DIM)
        qt = qt_ref[pl.ds(hoff, DIFF_V_DIM), pl.ds(off, tq)]
        k = k_ref[:, pl.ds(hoff, DIFF_V_DIM)]
        halves = [jnp.concatenate([qt[:DIFF_HEAD_DIM], zero_half], axis=0),
                  jnp.concatenate([zero_half, qt[DIFF_HEAD_DIM:]], axis=0)]
        for t in range(2):
            s = _dot(k, halves[t])
            s_ref[slot, t] = s
            m_ref[slot, t] = jnp.max(s, axis=0, keepdims=True)

    def softmax(slot):
        for t in range(2):
            a_ref[slot, t] = jnp.exp2(s_ref[slot, t] - m_ref[slot, t]).astype(BF16)

    def values(i, slot):
        head, off = locate(i)
        vt = vt_ref[pl.ds(pl.multiple_of(head * VT_ROWS, BF16_ROWS), VT_ROWS), :]
        halves = []
        for t in range(2):
            oa = _dot(vt, a_ref[slot, t])
            halves.append(oa[:DIFF_V_DIM] * (1.0 / oa[DIFF_V_DIM:DIFF_V_DIM + 1]))
        o = halves[0] - lam * halves[1]
        ms = jnp.sum(o * o, axis=0, keepdims=True) * (1.0 / DIFF_V_DIM)
        y = o * lax.rsqrt(ms + NORM_EPS) * gcol
        o_ref[pl.ds(off, tq), pl.ds(pl.multiple_of(head * DIFF_V_DIM, DIFF_V_DIM), DIFF_V_DIM)] = (
            y.T.astype(o_ref.dtype))

    scores(0, 0)
    scores(1, 1)
    softmax(0)

    def pair(j, carry):
        i = 2 * j + 1
        scores(i + 1, 0)
        values(i - 1, 0)
        softmax(1)
        scores(i + 2, 1)
        values(i, 1)
        softmax(0)
        return carry

    lax.fori_loop(0, (nblk - 2) // 2, pair, 0)
    values(nblk - 2, 0)
    softmax(1)
    values(nblk - 1, 1)


def _diffattn_call(qt, k, vt, lamv, gcol, lambda_init, seq, tq, hps):
    t = k.shape[0]
    assert (hps * seq // tq) % 2 == 0 and DIFF_HEADS % hps == 0
    col = lambda rows: pl.BlockSpec((hps * rows, seq), lambda i, h: (h, i))
    row = pl.BlockSpec((seq, hps * DIFF_V_DIM), lambda i, h: (i, h))
    return pl.pallas_call(
        functools.partial(_diffattn_kernel, lambda_init=lambda_init, tq=tq, hps=hps),
        grid=(t // seq, DIFF_HEADS // hps),
        in_specs=[col(DIFF_V_DIM), row, col(VT_ROWS), _const_spec(lamv.shape), _const_spec(gcol.shape)],
        out_specs=row,
        out_shape=jax.ShapeDtypeStruct((t, DIFF_WIDTH), BF16),
        scratch_shapes=[pltpu.VMEM((2, 2, seq, tq), F32),
                        pltpu.VMEM((2, 2, seq, tq), BF16),
                        pltpu.VMEM((2, 2, 1, tq), F32)],
        compiler_params=pltpu.CompilerParams(dimension_semantics=("arbitrary", "arbitrary"),
                                             vmem_limit_bytes=VMEM_LIMIT),
        name="diffattn",
    )(qt, k, vt, lamv, gcol)


def _tail_kernel(x1_ref, yf_ref, yb_ref, z_ref, yd_ref, kt_ref, vm_ref,
                 gssd_ref, wout_ref, gx_ref, wq_ref, wo_ref,
                 g2_ref, wg_ref, wu_ref, wd_ref, gfin_ref,
                 o_ref, acc_ref):
    y = (yf_ref[...] + yb_ref[...]) * _silu(z_ref[...])
    gw = SSD_WIDTH // SSD_GROUPS
    gssd = gssd_ref[...]
    yn = jnp.concatenate(
        [_rms(y[:, g * gw:(g + 1) * gw], gssd[:, g * gw:(g + 1) * gw]) for g in range(SSD_GROUPS)],
        axis=1).astype(BF16)
    yn = _permute_chunks(_chunk_perm(True), yn)
    x2 = x1_ref[...] + _dot(yn, wout_ref[0:SSD_WIDTH, :]) + _dot(yd_ref[...], wout_ref[SSD_WIDTH:, :])
    h = _rms(x2, gx_ref[...]).astype(BF16)
    qx = (_dot(h, wq_ref[...]) * (XATTN_HEAD_DIM ** -0.5)).astype(BF16)
    heads = []
    for hd in range(XATTN_HEADS):
        sl = slice(hd * XATTN_HEAD_DIM, (hd + 1) * XATTN_HEAD_DIM)
        s = _dot(qx[:, sl], kt_ref[sl, :])
        p = jnp.exp(s - jnp.max(s, axis=-1, keepdims=True))
        p = p * (1.0 / jnp.sum(p, axis=-1, keepdims=True))
        heads.append(_dot(p.astype(BF16), vm_ref[:, sl]).astype(BF16))
    x3 = x2 + _dot(jnp.concatenate(heads, axis=1), wo_ref[...])
    x4 = _swiglu_residual(x3, g2_ref, wg_ref, wu_ref, wd_ref, acc_ref)
    o_ref[...] = _rms(x4, gfin_ref[...])


def _tail_call(x1, yf, yb, z, yd, kt, vm, gssd, wout, gx, wq, wo, g2, wg, wu, wd, gfin, seq, tm):
    t, d = x1.shape
    nseq = seq // tm
    m = vm.shape[1]
    row = lambda w: pl.BlockSpec((tm, w), lambda i: (i, 0))
    return pl.pallas_call(
        _tail_kernel,
        grid=(t // tm,),
        in_specs=[row(d), row(SSD_WIDTH), row(SSD_WIDTH), row(SSD_WIDTH), row(DIFF_WIDTH),
                  pl.BlockSpec((None, d, m), lambda i: (i // nseq, 0, 0)),
                  pl.BlockSpec((None, m, d), lambda i: (i // nseq, 0, 0)),
                  _const_spec(gssd.shape), _const_spec(wout.shape), _const_spec(gx.shape),
                  _const_spec(wq.shape), _const_spec(wo.shape), _const_spec(g2.shape),
                  _const_spec(wg.shape), _const_spec(wu.shape), _const_spec(wd.shape),
                  _const_spec(gfin.shape)],
        out_specs=row(d),
        out_shape=jax.ShapeDtypeStruct((t, d), F32),
        scratch_shapes=[pltpu.VMEM((tm, d), F32)],
        compiler_params=pltpu.CompilerParams(dimension_semantics=("arbitrary",),
                                             vmem_limit_bytes=VMEM_LIMIT),
        name="tail",
    )(x1, yf, yb, z, yd, kt, vm, gssd, wout, gx, wq, wo, g2, wg, wu, wd, gfin)


def _rope_lane_tables(seq):
    pos = jnp.arange(seq, dtype=F32)
    inv_freq = 1.0 / (ROPE_THETA ** (jnp.arange(0, ROPE_DIM, 2, dtype=F32) / ROPE_DIM))
    ang = pos[:, None] * inv_freq[None, :]
    cos, sin = jnp.cos(ang), jnp.sin(ang)
    half = ROPE_DIM // 2
    pad = jnp.zeros((seq, DIFF_HEAD_DIM - ROPE_DIM), F32)
    zeros = jnp.zeros((seq, half), F32)
    cos64 = jnp.concatenate([cos, cos, pad + 1.0], axis=1)
    sa64 = jnp.concatenate([zeros, sin, pad], axis=1)
    sb64 = jnp.concatenate([-sin, zeros, pad], axis=1)
    tile = lambda a: jnp.concatenate([a, a], axis=1)
    return tile(cos64), tile(sa64), tile(sb64), cos.T, sin.T


def _ff_chunks(w_gate, w_up, w_down):
    d = w_gate.shape[0]
    wg = w_gate.astype(BF16).reshape(d, N_FF_CHUNKS, FF_CHUNK).transpose(1, 0, 2)
    wu = w_up.astype(BF16).reshape(d, N_FF_CHUNKS, FF_CHUNK).transpose(1, 0, 2)
    wd = w_down.astype(BF16).reshape(N_FF_CHUNKS, FF_CHUNK, d)
    return wg, wu, wd


def _tiling(seq):
    tm = min(512, seq)
    tq = 256
    hps = 2
    rc = min(4, seq // SSD_CHUNK)
    assert seq % tm == 0 and seq % tq == 0 and seq % (rc * SSD_CHUNK) == 0
    return tm, tq, hps, rc


def _pad_lanes(v, fill):
    return jnp.concatenate([v.astype(F32), jnp.full((LANES - v.shape[0],), fill, F32)])[None, :]


def kernel(x, mem, ffn1_norm_g, ffn1_w_gate, ffn1_w_up, ffn1_w_down, mix_norm_g, w_in, conv_w, conv_b, dt_bias_fwd, dt_bias_bwd, A_log_fwd, A_log_bwd, D_skip, ssd_norm_g, lambda_q1, lambda_k1, lambda_q2, lambda_k2, diff_subln_g, w_out, xattn_norm_g, mem_norm_g, xattn_w_q, xattn_w_kv, xattn_w_o, ffn2_norm_g, ffn2_w_gate, ffn2_w_up, ffn2_w_down, final_norm_g):
    b, s, d = x.shape
    depth = w_in.shape[0]
    assert depth == 1, "single-layer block only"
    tm, tq, hps, rc = _tiling(s)
    cos_t, sa_t, sb_t, cos_r, sin_r = _rope_lane_tables(s)
    xf = x.reshape(b * s, d)
    row = lambda v: v.astype(F32)[None, :]
    for layer in range(depth):
        lambda_init = 0.8 - 0.6 * math.exp(-0.3 * layer)
        w_kv = xattn_w_kv[layer]
        kt, vm = _kv_call(mem, row(mem_norm_g[layer]),
                          w_kv[:, :d].T.astype(BF16), w_kv[:, d:].astype(BF16))
        wg1, wu1, wd1 = _ff_chunks(ffn1_w_gate[layer], ffn1_w_up[layer], ffn1_w_down[layer])
        n_ssd_in = SSD_WIDTH + SSD_CONV_DIM + 2 * SSD_HEADS
        wl = w_in[layer]
        w_q = wl[:, n_ssd_in:n_ssd_in + DIFF_WIDTH]
        w_k = wl[:, n_ssd_in + DIFF_WIDTH:n_ssd_in + 2 * DIFF_WIDTH]
        w_v = wl[:, n_ssd_in + 2 * DIFF_WIDTH:]
        win = jnp.concatenate(
            [wl[:, :n_ssd_in], jnp.zeros((d, DT_PAD - 2 * SSD_HEADS), wl.dtype), w_k],
            axis=1).astype(BF16)
        x1, z, xbc, dt, qt, k, vt = _ffn_inproj_call(
            xf, row(ffn1_norm_g[layer]), wg1, wu1, wd1, row(mix_norm_g[layer]), win,
            w_q.T.astype(BF16), w_v.T.astype(BF16), cos_t, sa_t, sb_t, cos_r, sin_r, s, tm)
        cw = jnp.concatenate([conv_w[layer].astype(F32),
                              jnp.zeros((SUBLANES - SSD_CONV, SSD_CONV_DIM), F32)], axis=0)
        dtbias = _pad_lanes(jnp.concatenate([dt_bias_fwd[layer], dt_bias_bwd[layer]]), 0.0)
        alog = _pad_lanes(jnp.concatenate([A_log_fwd[layer], A_log_bwd[layer]]), -1e4)
        dexp = jnp.repeat(D_skip[layer].astype(F32), SSD_HEAD_DIM)[None, :]
        yf, yb = _ssd_call(xbc.reshape(b, s, SSD_CONV_DIM), dt.reshape(b, s, DT_PAD),
                           cw, row(conv_b[layer]), dtbias, alog, dexp, rc)
        lamv = jnp.stack([lambda_q1[layer], lambda_k1[layer],
                          lambda_q2[layer], lambda_k2[layer]]).astype(F32)
        yd = _diffattn_call(qt, k, vt, lamv, diff_subln_g[layer].astype(F32)[:, None],
                            lambda_init, s, tq, hps)
        wg2, wu2, wd2 = _ff_chunks(ffn2_w_gate[layer], ffn2_w_up[layer], ffn2_w_down[layer])
        gfin = row(final_norm_g)
        xf = _tail_call(x1, yf.reshape(b * s, SSD_WIDTH), yb.reshape(b * s, SSD_WIDTH), z,
                        yd, kt, vm,
                        row(ssd_norm_g[layer]), w_out[layer].astype(BF16), row(xattn_norm_g[layer]),
                        xattn_w_q[layer].astype(BF16), xattn_w_o[layer].astype(BF16),
                        row(ffn2_norm_g[layer]), wg2, wu2, wd2, gfin, s, tm)
    return xf.reshape(b, s, d)
```

```python
import functools
import math

import jax
import jax.numpy as jnp
from jax import lax
from jax.experimental import pallas as pl
from jax.experimental.pallas import tpu as pltpu

F32 = jnp.float32
BF16 = jnp.bfloat16

NORM_EPS = 1e-5
D_MODEL = 1024
D_FF = 2816
FF_CHUNK = 256
N_FF_CHUNKS = D_FF // FF_CHUNK
SSD_WIDTH = 512
SSD_HEADS = 8
SSD_HEAD_DIM = 64
SSD_GROUPS = 2
SSD_STATE = 128
SSD_CONV = 5
SSD_CHUNK = 128
SSD_CONV_DIM = 1024
DIFF_WIDTH = 512
DIFF_HEADS = 4
DIFF_HEAD_DIM = 64
DIFF_V_DIM = 128
ROPE_THETA = 500000.0
ROPE_DIM = 16
XATTN_HEADS = 4
XATTN_HEAD_DIM = 256
LANES = 128
SUBLANES = 8
BF16_ROWS = 2 * SUBLANES
VT_ROWS = DIFF_V_DIM + BF16_ROWS
DT_PAD = LANES
IN_Z, IN_XBC, IN_DT = 0, SSD_WIDTH, SSD_WIDTH + SSD_CONV_DIM
IN_K = IN_DT + DT_PAD
IN_TOTAL = IN_K + DIFF_WIDTH
VMEM_LIMIT = 56 * 1024 * 1024


def _dot(a, b):
    return jnp.dot(a, b, preferred_element_type=F32)


def _rms(xf, g):
    ms = jnp.sum(xf * xf, axis=-1, keepdims=True) * (1.0 / xf.shape[-1])
    return xf * lax.rsqrt(ms + NORM_EPS) * g


def _silu(x):
    return x * jax.nn.sigmoid(x)


def _swiglu_residual(x, g_ref, wg_ref, wu_ref, wd_ref, acc_ref):
    h = _rms(x, g_ref[...]).astype(BF16)
    acc_ref[...] = jnp.zeros_like(acc_ref)
    for c in range(N_FF_CHUNKS):
        cols = slice(c * FF_CHUNK, (c + 1) * FF_CHUNK)
        gate = _dot(h, wg_ref[:, cols])
        up = _dot(h, wu_ref[:, cols])
        act = (_silu(gate) * up).astype(BF16)
        acc_ref[...] += _dot(act, wd_ref[cols, :])
    return x + 0.5 * acc_ref[...]


def _const_spec(shape):
    nd = len(shape)
    return pl.BlockSpec(shape, lambda *_: (0,) * nd, pipeline_mode=pl.Buffered(1))


def _kv_kernel(mem_ref, g_ref, wkt_ref, wv_ref, kt_ref, v_ref):
    m = _rms(mem_ref[...], g_ref[...]).astype(BF16)
    kt = lax.dot_general(wkt_ref[...], m, (((1,), (1,)), ((), ())),
                         preferred_element_type=F32)
    kt_ref[...] = kt.astype(BF16)
    v_ref[...] = _dot(m, wv_ref[...]).astype(BF16)


def _kv_call(mem, g, wkt, wv):
    b, m, d = mem.shape
    return pl.pallas_call(
        _kv_kernel,
        grid=(b,),
        in_specs=[pl.BlockSpec((None, m, d), lambda i: (i, 0, 0)),
                  _const_spec((1, d)), _const_spec((d, d)), _const_spec((d, d))],
        out_specs=[pl.BlockSpec((None, d, m), lambda i: (i, 0, 0)),
                   pl.BlockSpec((None, m, d), lambda i: (i, 0, 0))],
        out_shape=[jax.ShapeDtypeStruct((b, d, m), BF16), jax.ShapeDtypeStruct((b, m, d), BF16)],
        compiler_params=pltpu.CompilerParams(dimension_semantics=("arbitrary",),
                                             vmem_limit_bytes=VMEM_LIMIT),
        name="kv",
    )(mem, g, wkt, wv)


def _rope(t, cos, sa, sb):
    return t * cos + pltpu.roll(t, 8, axis=1) * sa + pltpu.roll(t, LANES - 8, axis=1) * sb


def _dot_nt(a, b):
    return lax.dot_general(a, b, (((1,), (1,)), ((), ())), preferred_element_type=F32)


def _rope_rows(tt, cos, sin):
    half = ROPE_DIM // 2
    pieces = []
    for base in range(0, DIFF_WIDTH, DIFF_HEAD_DIM):
        t1 = tt[base:base + half]
        t2 = tt[base + half:base + ROPE_DIM]
        pieces += [t1 * cos - t2 * sin, t2 * cos + t1 * sin, tt[base + ROPE_DIM:base + DIFF_HEAD_DIM]]
    return jnp.concatenate(pieces, axis=0)


def _ffn_inproj_kernel(x_ref, g1_ref, wg_ref, wu_ref, wd_ref, gmix_ref, win_ref, wqt_ref, wvt_ref,
                       cos_ref, sa_ref, sb_ref, cosr_ref, sinr_ref,
                       x1_ref, z_ref, xbc_ref, dt_ref, qt_ref, k_ref, vt_ref, acc_ref):
    x1 = _swiglu_residual(x_ref[...], g1_ref, wg_ref, wu_ref, wd_ref, acc_ref)
    x1_ref[...] = x1
    h = _rms(x1, gmix_ref[...]).astype(BF16)
    hp = _permute_chunks(_chunk_perm(False), h)
    z_ref[...] = _dot(hp, win_ref[:, IN_Z:IN_XBC])
    xbc_ref[...] = _dot(hp, win_ref[:, IN_XBC:IN_DT])
    dt_ref[...] = _dot(hp, win_ref[:, IN_DT:IN_K])
    vt = _dot_nt(wvt_ref[...], h).astype(BF16)
    ones = jnp.ones((BF16_ROWS, vt.shape[1]), BF16)
    for hd in range(DIFF_HEADS):
        vt_ref[hd * VT_ROWS:hd * VT_ROWS + DIFF_V_DIM, :] = vt[hd * DIFF_V_DIM:(hd + 1) * DIFF_V_DIM]
        vt_ref[hd * VT_ROWS + DIFF_V_DIM:(hd + 1) * VT_ROWS, :] = ones
    qt = _dot_nt(wqt_ref[...], h) * (DIFF_HEAD_DIM ** -0.5 * math.log2(math.e))
    qt_ref[...] = _rope_rows(qt, cosr_ref[...], sinr_ref[...]).astype(BF16)
    cos, sa, sb = cos_ref[...], sa_ref[...], sb_ref[...]
    k = _dot(h, win_ref[:, IN_K:IN_TOTAL])
    for s in range(DIFF_WIDTH // LANES):
        sl = slice(s * LANES, (s + 1) * LANES)
        k_ref[:, sl] = _rope(k[:, sl], cos, sa, sb).astype(BF16)


def _ffn_inproj_call(xf, g1, wg, wu, wd, gmix, win, wqt, wvt, cos_t, sa_t, sb_t, cos_r, sin_r, seq, tm):
    t, d = xf.shape
    nseq = seq // tm
    row = lambda w: pl.BlockSpec((tm, w), lambda i: (i, 0))
    colb = lambda rows: pl.BlockSpec((rows, tm), lambda i: (0, i))
    tab = pl.BlockSpec((tm, LANES), lambda i: (i % nseq, 0))
    tabr = pl.BlockSpec((ROPE_DIM // 2, tm), lambda i: (0, i % nseq))
    return pl.pallas_call(
        _ffn_inproj_kernel,
        grid=(t // tm,),
        in_specs=[row(d), _const_spec((1, d)),
                  _const_spec(wg.shape), _const_spec(wu.shape), _const_spec(wd.shape),
                  _const_spec((1, d)), _const_spec(win.shape), _const_spec(wqt.shape),
                  _const_spec(wvt.shape), tab, tab, tab, tabr, tabr],
        out_specs=[row(d), row(SSD_WIDTH), row(SSD_CONV_DIM), row(DT_PAD),
                   colb(DIFF_WIDTH), row(DIFF_WIDTH), colb(DIFF_HEADS * VT_ROWS)],
        out_shape=[jax.ShapeDtypeStruct((t, d), F32),
                   jax.ShapeDtypeStruct((t, SSD_WIDTH), F32),
                   jax.ShapeDtypeStruct((t, SSD_CONV_DIM), F32),
                   jax.ShapeDtypeStruct((t, DT_PAD), F32),
                   jax.ShapeDtypeStruct((DIFF_WIDTH, t), BF16),
                   jax.ShapeDtypeStruct((t, DIFF_WIDTH), BF16),
                   jax.ShapeDtypeStruct((DIFF_HEADS * VT_ROWS, t), BF16)],
        scratch_shapes=[pltpu.VMEM((tm, d), F32)],
        compiler_params=pltpu.CompilerParams(dimension_semantics=("arbitrary",),
                                             vmem_limit_bytes=VMEM_LIMIT),
        name="ffn_inproj",
    )(xf, g1, wg, wu, wd, gmix, win, wqt, wvt, cos_t, sa_t, sb_t, cos_r, sin_r)


def _pair_expand(tile, c0):
    rows = tile.shape[0]
    lane = lax.broadcasted_iota(jnp.int32, (rows, LANES), 1)
    lo = jnp.broadcast_to(tile[:, c0:c0 + 1], (rows, LANES))
    hi = jnp.broadcast_to(tile[:, c0 + 1:c0 + 2], (rows, LANES))
    return jnp.where(lane < SSD_HEAD_DIM, lo, hi)


def _chunk_time(r):
    nph = SSD_CHUNK // SUBLANES
    return jnp.right_shift(r, SUBLANES.bit_length() - 1) + (r & (SUBLANES - 1)) * nph


def _chunk_perm(transpose):
    row = lax.broadcasted_iota(jnp.int32, (SSD_CHUNK, SSD_CHUNK), 0)
    col = lax.broadcasted_iota(jnp.int32, (SSD_CHUNK, SSD_CHUNK), 1)
    hit = (row == _chunk_time(col)) if transpose else (col == _chunk_time(row))
    return jnp.where(hit, 1.0, 0.0).astype(BF16)


def _permute_chunks(perm, v):
    return jnp.concatenate(
        [_dot(perm, v[c:c + SSD_CHUNK]).astype(BF16) for c in range(0, v.shape[0], SSD_CHUNK)], axis=0)


def _chunk_cumsum(a, d):
    nph = SSD_CHUNK // SUBLANES
    slabs = [a[p * SUBLANES:(p + 1) * SUBLANES] for p in range(nph)]
    part = [None] * nph
    run = None
    for p in (range(nph) if d == 0 else reversed(range(nph))):
        run = slabs[p] if run is None else run + slabs[p]
        part[p] = run
    sub = lax.broadcasted_iota(jnp.int32, run.shape, 0)
    incl = run
    shift = 1
    while shift < SUBLANES:
        if d == 0:
            incl = incl + jnp.where(sub >= shift, pltpu.roll(incl, shift, axis=0), 0.0)
        else:
            incl = incl + jnp.where(sub < SUBLANES - shift, pltpu.roll(incl, SUBLANES - shift, axis=0), 0.0)
        shift *= 2
    off = incl - run
    acum = jnp.concatenate([part[p] + off for p in range(nph)], axis=0)
    tot = incl[SUBLANES - 1:SUBLANES] if d == 0 else incl[0:1]
    return acum, tot


def _ssd_decay_terms(dtraw, dtbias, a_neg, d):
    v = dtraw + dtbias
    dt = jnp.maximum(v, 0.0) + jnp.log1p(jnp.exp(-jnp.abs(v)))
    acum, tot = _chunk_cumsum(dt * a_neg, d)
    return dict(acum=acum, acum_t=acum.T, dt_t=dt.T, eac=jnp.exp(acum),
                wl=dt * jnp.exp(tot - acum), cdec=jnp.exp(tot))


def _ssd_chunk(xc, terms, state_ref, d, with_skip, dexp):
    L = SSD_CHUNK
    row = _chunk_time(lax.broadcasted_iota(jnp.int32, (L, L), 0))
    col = _chunk_time(lax.broadcasted_iota(jnp.int32, (L, L), 1))
    xs = xc[:, :SSD_WIDTH]
    valid = row >= col if d == 0 else row <= col
    acum, acum_t, dt_t = terms["acum"], terms["acum_t"], terms["dt_t"]
    eac, wl, cdec = terms["eac"], terms["wl"], terms["cdec"]
    lane2 = lax.broadcasted_iota(jnp.int32, (2 * L, LANES), 1)
    row2 = lax.broadcasted_iota(jnp.int32, (2 * L, LANES), 0)
    bd_mask = (jnp.right_shift(row2, L.bit_length() - 1)
               == jnp.right_shift(lane2, SSD_HEAD_DIM.bit_length() - 1))
    ys = []
    for g in range(SSD_GROUPS):
        bg = xc[:, SSD_WIDTH + g * SSD_STATE:SSD_WIDTH + (g + 1) * SSD_STATE]
        cg = xc[:, SSD_WIDTH + (SSD_GROUPS + g) * SSD_STATE:SSD_WIDTH + (SSD_GROUPS + g + 1) * SSD_STATE]
        bt = bg.T.astype(BF16)
        cb16 = cg.astype(BF16)
        cb = _dot(cb16, bt)
        prev = state_ref[d, g]
        yoff = _dot(cb16, prev.astype(BF16))
        w_parts, cd_parts = [], []
        for p in range(2):
            h0 = 4 * g + 2 * p
            c0 = SSD_HEADS * d + h0
            ms = []
            for c in (c0, c0 + 1):
                diff = acum[:, c:c + 1] - acum_t[c:c + 1, :]
                dec = jnp.exp(jnp.where(valid, diff, -jnp.inf))
                ms.append((cb * dec * dt_t[c:c + 1, :]).astype(BF16))
            lhs = jnp.concatenate(ms, axis=1)
            xp = xs[:, h0 * SSD_HEAD_DIM:(h0 + 2) * SSD_HEAD_DIM]
            xp2 = jnp.concatenate([xp, xp], axis=0)
            rhs = jnp.where(bd_mask, xp2, 0.0).astype(BF16)
            ydiag = _dot(lhs, rhs)
            y = ydiag + yoff[:, p * LANES:(p + 1) * LANES] * _pair_expand(eac, c0)
            if with_skip:
                y = y + dexp[:, h0 * SSD_HEAD_DIM:(h0 + 2) * SSD_HEAD_DIM] * xp
            ys.append(y)
            w_parts.append((xp * _pair_expand(wl, c0)).astype(BF16))
            cd_parts.append(_pair_expand(cdec, c0))
        w = jnp.concatenate(w_parts, axis=1)
        cd = jnp.concatenate(cd_parts, axis=1)
        state_ref[d, g] = prev * cd + _dot(bt, w)
    return jnp.concatenate(ys, axis=1)


def _ssd_kernel(xf_ref, xfp_ref, xfn_ref, xb_ref, xbp_ref, xbn_ref, dtf_ref, dtb_ref,
                cw_ref, cbias_ref, dtbias_ref, alog_ref, dexp_ref,
                yf_ref, yb_ref, state_ref, *, rc):
    j = pl.program_id(1)
    nb = pl.num_programs(1)
    nph = SSD_CHUNK // SUBLANES
    pad = SSD_CONV // 2
    halo = pad * SUBLANES

    @pl.when(j == 0)
    def _():
        state_ref[...] = jnp.zeros_like(state_ref)

    a_neg = -jnp.exp(alog_ref[...])
    dtbias = dtbias_ref[...]
    dexp = dexp_ref[...]
    cdim = cw_ref.shape[1]
    taps = [jnp.broadcast_to(cw_ref[k:k + 1, :], (SUBLANES, cdim)) for k in range(SSD_CONV)]
    bias = jnp.broadcast_to(cbias_ref[...], (SUBLANES, cdim))
    sub = lax.broadcasted_iota(jnp.int32, (SUBLANES, cdim), 0)

    def conv_chunk(x_ref, prev_ref, next_ref, blk, c):
        lo = c * SSD_CHUNK
        slab = lambda p: x_ref[lo + p * SUBLANES:lo + (p + 1) * SUBLANES, :]
        if c > 0:
            before = x_ref[lo - halo:lo, :]
        else:
            before = jnp.where(blk > 0, prev_ref[...], 0.0)
        if c < rc - 1:
            after = x_ref[lo + SSD_CHUNK:lo + SSD_CHUNK + halo, :]
        else:
            after = jnp.where(blk < nb - 1, next_ref[...], 0.0)
        window = [None] * (nph + 2 * pad)
        for p in range(nph):
            window[p + pad] = slab(p)
        for q in range(pad):
            src = nph - pad + q
            prev_slab = before[q * SUBLANES:(q + 1) * SUBLANES, :]
            window[q] = jnp.where(sub == 0, pltpu.roll(prev_slab, 1, axis=0), pltpu.roll(slab(src), 1, axis=0))
            next_slab = after[q * SUBLANES:(q + 1) * SUBLANES, :]
            window[nph + pad + q] = jnp.where(sub == SUBLANES - 1,
                                              pltpu.roll(next_slab, SUBLANES - 1, axis=0),
                                              pltpu.roll(slab(q), SUBLANES - 1, axis=0))
        out = []
        for p in range(nph):
            acc = bias
            for k in range(SSD_CONV):
                acc = acc + window[p + k] * taps[k]
            out.append(_silu(acc))
        return jnp.concatenate(out, axis=0)

    rows_of = lambda c: slice(c * SSD_CHUNK, (c + 1) * SSD_CHUNK)
    terms_f = [_ssd_decay_terms(dtf_ref[rows_of(c), :], dtbias, a_neg, 0) for c in range(rc)]
    terms_b = [_ssd_decay_terms(dtb_ref[rows_of(c), :], dtbias, a_neg, 1) for c in range(rc)]
    xc_f = [conv_chunk(xf_ref, xfp_ref, xfn_ref, j, c) for c in range(rc)]
    xc_b = [conv_chunk(xb_ref, xbp_ref, xbn_ref, nb - 1 - j, c) for c in range(rc)]
    for c in range(rc):
        cr = rc - 1 - c
        yf_ref[rows_of(c), :] = _ssd_chunk(xc_f[c], terms_f[c], state_ref, 0, True, dexp)
        yb_ref[rows_of(cr), :] = _ssd_chunk(xc_b[cr], terms_b[cr], state_ref, 1, False, dexp)


def _ssd_call(xbc, dt, cw, cbias, dtbias, alog, dexp, rc):
    b, s, cdim = xbc.shape
    rows = rc * SSD_CHUNK
    nb = s // rows
    halo_rows = (SSD_CONV // 2) * SUBLANES
    hb = rows // halo_rows
    nh = s // halo_rows
    main = lambda w, rev: pl.BlockSpec(
        (None, rows, w), (lambda i, j: (i, nb - 1 - j, 0)) if rev else (lambda i, j: (i, j, 0)))

    def halo(rev, nxt):
        def idx(i, j):
            blk = nb - 1 - j if rev else j
            h = (blk + 1) * hb if nxt else blk * hb - 1
            return (i, jnp.clip(h, 0, nh - 1), 0)
        return pl.BlockSpec((None, halo_rows, cdim), idx)

    return pl.pallas_call(
        functools.partial(_ssd_kernel, rc=rc),
        grid=(b, nb),
        in_specs=[main(cdim, False), halo(False, False), halo(False, True),
                  main(cdim, True), halo(True, False), halo(True, True),
                  main(DT_PAD, False), main(DT_PAD, True),
                  _const_spec(cw.shape), _const_spec(cbias.shape), _const_spec(dtbias.shape),
                  _const_spec(alog.shape), _const_spec(dexp.shape)],
        out_specs=[main(SSD_WIDTH, False), main(SSD_WIDTH, True)],
        out_shape=[jax.ShapeDtypeStruct((b, s, SSD_WIDTH), F32),
                   jax.ShapeDtypeStruct((b, s, SSD_WIDTH), F32)],
        scratch_shapes=[pltpu.VMEM((2, SSD_GROUPS, SSD_STATE, 4 * SSD_HEAD_DIM), F32)],
        compiler_params=pltpu.CompilerParams(dimension_semantics=("arbitrary", "arbitrary"),
                                             vmem_limit_bytes=VMEM_LIMIT),
        name="ssd",
    )(xbc, xbc, xbc, xbc, xbc, xbc, dt, dt, cw, cbias, dtbias, alog, dexp)


def _diffattn_kernel(qt_ref, k_ref, vt_ref, lam_ref, g_ref, o_ref, s_ref, a_ref, m_ref,
                     *, lambda_init, tq, hps):
    seq = k_ref.shape[0]
    lamv = lam_ref[...]
    l1 = jnp.sum(lamv[0:1] * lamv[1:2], axis=-1, keepdims=True)
    l2 = jnp.sum(lamv[2:3] * lamv[3:4], axis=-1, keepdims=True)
    lam = jnp.exp(l1) - jnp.exp(l2) + lambda_init
    gcol = g_ref[...] * (1.0 - lambda_init)
    nq = seq // tq
    nblk = hps * nq
    zero_half = jnp.zeros((DIFF_HEAD_DIM, tq), BF16)

    def locate(i):
        head = i // nq
        return head, pl.multiple_of((i - head * nq) * tq, tq)

    def scores(i, slot):
        head, off = locate(i)
        hoff = pl.multiple_of(head * DIFF_V_DIM, DIFF_V_DIM)
        qt = qt_ref[pl.ds(hoff, DIFF_V_DIM), pl.ds(off, tq)]
        k = k_ref[:, pl.ds(hoff, DIFF_V_DIM)]
        halves = [jnp.concatenate([qt[:DIFF_HEAD_DIM], zero_half], axis=0),
                  jnp.concatenate([zero_half, qt[DIFF_HEAD_DIM:]], axis=0)]
        for t in range(2):
            s = _dot(k, halves[t])
            s_ref[slot, t] = s
            m_ref[slot, t] = jnp.max(s, axis=0, keepdims=True)

    def softmax(slot):
        for t in range(2):
            a_ref[slot, t] = jnp.exp2(s_ref[slot, t] - m_ref[slot, t]).astype(BF16)

    def values(i, slot):
        head, off = locate(i)
        vt = vt_ref[pl.ds(pl.multiple_of(head * VT_ROWS, BF16_ROWS), VT_ROWS), :]
        halves = []
        for t in range(2):
            oa = _dot(vt, a_ref[slot, t])
            halves.append(oa[:DIFF_V_DIM] * (1.0 / oa[DIFF_V_DIM:DIFF_V_DIM + 1]))
        o = halves[0] - lam * halves[1]
        ms = jnp.sum(o * o, axis=0, keepdims=True) * (1.0 / DIFF_V_DIM)
        y = o * lax.rsqrt(ms + NORM_EPS) * gcol
        o_ref[pl.ds(off, tq), pl.ds(pl.multiple_of(head * DIFF_V_DIM, DIFF_V_DIM), DIFF_V_DIM)] = (
            y.T.astype(o_ref.dtype))

    scores(0, 0)
    scores(1, 1)
    softmax(0)

    def pair(j, carry):
        i = 2 * j + 1
        scores(i + 1, 0)
        values(i - 1, 0)
        softmax(1)
        scores(i + 2, 1)
        values(i, 1)
        softmax(0)
        return carry

    lax.fori_loop(0, (nblk - 2) // 2, pair, 0)
    values(nblk - 2, 0)
    softmax(1)
    values(nblk - 1, 1)


def _diffattn_call(qt, k, vt, lamv, gcol, lambda_init, seq, tq, hps):
    t = k.shape[0]
    assert (hps * seq // tq) % 2 == 0 and DIFF_HEADS % hps == 0
    col = lambda rows: pl.BlockSpec((hps * rows, seq), lambda i, h: (h, i))
    row = pl.BlockSpec((seq, hps * DIFF_V_DIM), lambda i, h: (i, h))
    return pl.pallas_call(
        functools.partial(_diffattn_kernel, lambda_init=lambda_init, tq=tq, hps=hps),
        grid=(t // seq, DIFF_HEADS // hps),
        in_specs=[col(DIFF_V_DIM), row, col(VT_ROWS), _const_spec(lamv.shape), _const_spec(gcol.shape)],
        out_specs=row,
        out_shape=jax.ShapeDtypeStruct((t, DIFF_WIDTH), BF16),
        scratch_shapes=[pltpu.VMEM((2, 2, seq, tq), F32),
                        pltpu.VMEM((2, 2, seq, tq), BF16),
                        pltpu.VMEM((2, 2, 1, tq), F32)],
        compiler_params=pltpu.CompilerParams(dimension_semantics=("arbitrary", "arbitrary"),
                                             vmem_limit_bytes=VMEM_LIMIT),
        name="diffattn",
    )(qt, k, vt, lamv, gcol)


def _tail_kernel(x1_ref, yf_ref, yb_ref, z_ref, yd_ref, kt_ref, vm_ref,
                 gssd_ref, wout_ref, gx_ref, wq_ref, wo_ref,
                 g2_ref, wg_ref, wu_ref, wd_ref, gfin_ref,
                 o_ref, acc_ref):
    y = (yf_ref[...] + yb_ref[...]) * _silu(z_ref[...])
    gw = SSD_WIDTH // SSD_GROUPS
    gssd = gssd_ref[...]
    yn = jnp.concatenate(
        [_rms(y[:, g * gw:(g + 1) * gw], gssd[:, g * gw:(g + 1) * gw]) for g in range(SSD_GROUPS)],
        axis=1).astype(BF16)
    yn = _permute_chunks(_chunk_perm(True), yn)
    x2 = x1_ref[...] + _dot(yn, wout_ref[0:SSD_WIDTH, :]) + _dot(yd_ref[...], wout_ref[SSD_WIDTH:, :])
    h = _rms(x2, gx_ref[...]).astype(BF16)
    qx = (_dot(h, wq_ref[...]) * (XATTN_HEAD_DIM ** -0.5)).astype(BF16)
    heads = []
    for hd in range(XATTN_HEADS):
        sl = slice(hd * XATTN_HEAD_DIM, (hd + 1) * XATTN_HEAD_DIM)
        s = _dot(qx[:, sl], kt_ref[sl, :])
        p = jnp.exp(s - jnp.max(s, axis=-1, keepdims=True))
        p = p * (1.0 / jnp.sum(p, axis=-1, keepdims=True))
        heads.append(_dot(p.astype(BF16), vm_ref[:, sl]).astype(BF16))
    x3 = x2 + _dot(jnp.concatenate(heads, axis=1), wo_ref[...])
    x4 = _swiglu_residual(x3, g2_ref, wg_ref, wu_ref, wd_ref, acc_ref)
    o_ref[...] = _rms(x4, gfin_ref[...])


def _tail_call(x1, yf, yb, z, yd, kt, vm, gssd, wout, gx, wq, wo, g2, wg, wu, wd, gfin, seq, tm):
    t, d = x1.shape
    nseq = seq // tm
    m = vm.shape[1]
    row = lambda w: pl.BlockSpec((tm, w), lambda i: (i, 0))
    return pl.pallas_call(
        _tail_kernel,
        grid=(t // tm,),
        in_specs=[row(d), row(SSD_WIDTH), row(SSD_WIDTH), row(SSD_WIDTH), row(DIFF_WIDTH),
                  pl.BlockSpec((None, d, m), lambda i: (i // nseq, 0, 0)),
                  pl.BlockSpec((None, m, d), lambda i: (i // nseq, 0, 0)),
                  _const_spec(gssd.shape), _const_spec(wout.shape), _const_spec(gx.shape),
                  _const_spec(wq.shape), _const_spec(wo.shape), _const_spec(g2.shape),
                  _const_spec(wg.shape), _const_spec(wu.shape), _const_spec(wd.shape),
                  _const_spec(gfin.shape)],
        out_specs=row(d),
        out_shape=jax.ShapeDtypeStruct((t, d), F32),
        scratch_shapes=[pltpu.VMEM((tm, d), F32)],
        compiler_params=pltpu.CompilerParams(dimension_semantics=("arbitrary",),
                                             vmem_limit_bytes=VMEM_LIMIT),
        name="tail",
    )(x1, yf, yb, z, yd, kt, vm, gssd, wout, gx, wq, wo, g2, wg, wu, wd, gfin)


def _rope_lane_tables(seq):
    pos = jnp.arange(seq, dtype=F32)
    inv_freq = 1.0 / (ROPE_THETA ** (jnp.arange(0, ROPE_DIM, 2, dtype=F32) / ROPE_DIM))
    ang = pos[:, None] * inv_freq[None, :]
    cos, sin = jnp.cos(ang), jnp.sin(ang)
    half = ROPE_DIM // 2
    pad = jnp.zeros((seq, DIFF_HEAD_DIM - ROPE_DIM), F32)
    zeros = jnp.zeros((seq, half), F32)
    cos64 = jnp.concatenate([cos, cos, pad + 1.0], axis=1)
    sa64 = jnp.concatenate([zeros, sin, pad], axis=1)
    sb64 = jnp.concatenate([-sin, zeros, pad], axis=1)
    tile = lambda a: jnp.concatenate([a, a], axis=1)
    return tile(cos64), tile(sa64), tile(sb64), cos.T, sin.T


def _ff_chunks(w_gate, w_up, w_down):
    return w_gate.astype(BF16), w_up.astype(BF16), w_down.astype(BF16)


def _tiling(seq):
    tm = min(512, seq)
    tq = 256
    hps = 2
    rc = min(4, seq // SSD_CHUNK)
    assert seq % tm == 0 and seq % tq == 0 and seq % (rc * SSD_CHUNK) == 0
    return tm, tq, hps, rc


def _pad_lanes(v, fill):
    return jnp.concatenate([v.astype(F32), jnp.full((LANES - v.shape[0],), fill, F32)])[None, :]


def kernel(x, mem, ffn1_norm_g, ffn1_w_gate, ffn1_w_up, ffn1_w_down, mix_norm_g, w_in, conv_w, conv_b, dt_bias_fwd, dt_bias_bwd, A_log_fwd, A_log_bwd, D_skip, ssd_norm_g, lambda_q1, lambda_k1, lambda_q2, lambda_k2, diff_subln_g, w_out, xattn_norm_g, mem_norm_g, xattn_w_q, xattn_w_kv, xattn_w_o, ffn2_norm_g, ffn2_w_gate, ffn2_w_up, ffn2_w_down, final_norm_g):
    b, s, d = x.shape
    depth = w_in.shape[0]
    assert depth == 1, "single-layer block only"
    tm, tq, hps, rc = _tiling(s)
    cos_t, sa_t, sb_t, cos_r, sin_r = _rope_lane_tables(s)
    xf = x.reshape(b * s, d)
    row = lambda v: v.astype(F32)[None, :]
    for layer in range(depth):
        lambda_init = 0.8 - 0.6 * math.exp(-0.3 * layer)
        w_kv = xattn_w_kv[layer]
        kt, vm = _kv_call(mem, row(mem_norm_g[layer]),
                          w_kv[:, :d].T.astype(BF16), w_kv[:, d:].astype(BF16))
        wg1, wu1, wd1 = _ff_chunks(ffn1_w_gate[layer], ffn1_w_up[layer], ffn1_w_down[layer])
        n_ssd_in = SSD_WIDTH + SSD_CONV_DIM + 2 * SSD_HEADS
        wl = w_in[layer]
        w_q = wl[:, n_ssd_in:n_ssd_in + DIFF_WIDTH]
        w_k = wl[:, n_ssd_in + DIFF_WIDTH:n_ssd_in + 2 * DIFF_WIDTH]
        w_v = wl[:, n_ssd_in + 2 * DIFF_WIDTH:]
        win = jnp.concatenate(
            [wl[:, :n_ssd_in], jnp.zeros((d, DT_PAD - 2 * SSD_HEADS), wl.dtype), w_k],
            axis=1).astype(BF16)
        x1, z, xbc, dt, qt, k, vt = _ffn_inproj_call(
            xf, row(ffn1_norm_g[layer]), wg1, wu1, wd1, row(mix_norm_g[layer]), win,
            w_q.T.astype(BF16), w_v.T.astype(BF16), cos_t, sa_t, sb_t, cos_r, sin_r, s, tm)
        cw = jnp.concatenate([conv_w[layer].astype(F32),
                              jnp.zeros((SUBLANES - SSD_CONV, SSD_CONV_DIM), F32)], axis=0)
        dtbias = _pad_lanes(jnp.concatenate([dt_bias_fwd[layer], dt_bias_bwd[layer]]), 0.0)
        alog = _pad_lanes(jnp.concatenate([A_log_fwd[layer], A_log_bwd[layer]]), -1e4)
        dexp = jnp.repeat(D_skip[layer].astype(F32), SSD_HEAD_DIM)[None, :]
        yf, yb = _ssd_call(xbc.reshape(b, s, SSD_CONV_DIM), dt.reshape(b, s, DT_PAD),
                           cw, row(conv_b[layer]), dtbias, alog, dexp, rc)
        lamv = jnp.stack([lambda_q1[layer], lambda_k1[layer],
                          lambda_q2[layer], lambda_k2[layer]]).astype(F32)
        yd = _diffattn_call(qt, k, vt, lamv, diff_subln_g[layer].astype(F32)[:, None],
                            lambda_init, s, tq, hps)
        wg2, wu2, wd2 = _ff_chunks(ffn2_w_gate[layer], ffn2_w_up[layer], ffn2_w_down[layer])
        gfin = row(final_norm_g)
        xf = _tail_call(x1, yf.reshape(b * s, SSD_WIDTH), yb.reshape(b * s, SSD_WIDTH), z,
                        yd, kt, vm,
                        row(ssd_norm_g[layer]), w_out[layer].astype(BF16), row(xattn_norm_g[layer]),
                        xattn_w_q[layer].astype(BF16), xattn_w_o[layer].astype(BF16),
                        row(ffn2_norm_g[layer]), wg2, wu2, wd2, gfin, s, tm)
    return xf.reshape(b, s, d)
```

```python
import functools
import math

import jax
import jax.numpy as jnp
from jax import lax
from jax.experimental import pallas as pl
from jax.experimental.pallas import tpu as pltpu

F32 = jnp.float32
BF16 = jnp.bfloat16

NORM_EPS = 1e-5
LOG2E = math.log2(math.e)
D_MODEL = 1024
D_FF = 2816
FF_CHUNK = 256
N_FF_CHUNKS = D_FF // FF_CHUNK
SSD_WIDTH = 512
SSD_HEADS = 8
SSD_HEAD_DIM = 64
SSD_GROUPS = 2
SSD_STATE = 128
SSD_CONV = 5
SSD_CHUNK = 128
SSD_CONV_DIM = 1024
DIFF_WIDTH = 512
DIFF_HEADS = 4
DIFF_HEAD_DIM = 64
DIFF_V_DIM = 128
ROPE_THETA = 500000.0
ROPE_DIM = 16
XATTN_HEADS = 4
XATTN_HEAD_DIM = 256
LANES = 128
SUBLANES = 8
BF16_ROWS = 2 * SUBLANES
VT_ROWS = DIFF_V_DIM + BF16_ROWS
DT_PAD = LANES
IN_Z, IN_XBC, IN_DT = 0, SSD_WIDTH, SSD_WIDTH + SSD_CONV_DIM
IN_K = IN_DT + DT_PAD
IN_TOTAL = IN_K + DIFF_WIDTH
VMEM_LIMIT = 56 * 1024 * 1024


def _dot(a, b):
    return jnp.dot(a, b, preferred_element_type=F32)


def _rms(xf, g):
    ms = jnp.sum(xf * xf, axis=-1, keepdims=True) * (1.0 / xf.shape[-1])
    return xf * lax.rsqrt(ms + NORM_EPS) * g


def _silu(x):
    return x * jax.nn.sigmoid(x)


def _swiglu_residual(x, g_ref, wg_ref, wu_ref, wd_ref, acc_ref):
    h = _rms(x, g_ref[...]).astype(BF16)
    acc_ref[...] = jnp.zeros_like(acc_ref)
    for c in range(N_FF_CHUNKS):
        cols = slice(c * FF_CHUNK, (c + 1) * FF_CHUNK)
        gate = _dot(h, wg_ref[:, cols])
        up = _dot(h, wu_ref[:, cols])
        act = (_silu(gate) * up).astype(BF16)
        acc_ref[...] += _dot(act, wd_ref[cols, :])
    return x + 0.5 * acc_ref[...]


def _const_spec(shape):
    nd = len(shape)
    return pl.BlockSpec(shape, lambda *_: (0,) * nd, pipeline_mode=pl.Buffered(1))


def _kv_kernel(mem_ref, g_ref, wkt_ref, wv_ref, kt_ref, v_ref):
    m = _rms(mem_ref[...], g_ref[...]).astype(BF16)
    kt = lax.dot_general(wkt_ref[...], m, (((1,), (1,)), ((), ())),
                         preferred_element_type=F32)
    kt_ref[...] = kt.astype(BF16)
    v_ref[...] = _dot(m, wv_ref[...]).astype(BF16)


def _kv_call(mem, g, wkt, wv):
    b, m, d = mem.shape
    return pl.pallas_call(
        _kv_kernel,
        grid=(b,),
        in_specs=[pl.BlockSpec((None, m, d), lambda i: (i, 0, 0)),
                  _const_spec((1, d)), _const_spec((d, d)), _const_spec((d, d))],
        out_specs=[pl.BlockSpec((None, d, m), lambda i: (i, 0, 0)),
                   pl.BlockSpec((None, m, d), lambda i: (i, 0, 0))],
        out_shape=[jax.ShapeDtypeStruct((b, d, m), BF16), jax.ShapeDtypeStruct((b, m, d), BF16)],
        compiler_params=pltpu.CompilerParams(dimension_semantics=("arbitrary",),
                                             vmem_limit_bytes=VMEM_LIMIT),
        name="kv",
    )(mem, g, wkt, wv)


def _rope(t, cos, sa, sb):
    return t * cos + pltpu.roll(t, 8, axis=1) * sa + pltpu.roll(t, LANES - 8, axis=1) * sb


def _dot_nt(a, b):
    return lax.dot_general(a, b, (((1,), (1,)), ((), ())), preferred_element_type=F32)


def _rope_rows(tt, cos, sin):
    half = ROPE_DIM // 2
    pieces = []
    for base in range(0, DIFF_WIDTH, DIFF_HEAD_DIM):
        t1 = tt[base:base + half]
        t2 = tt[base + half:base + ROPE_DIM]
        pieces += [t1 * cos - t2 * sin, t2 * cos + t1 * sin, tt[base + ROPE_DIM:base + DIFF_HEAD_DIM]]
    return jnp.concatenate(pieces, axis=0)


def _ffn_inproj_kernel(x_ref, g1_ref, wg_ref, wu_ref, wd_ref, gmix_ref, win_ref, wqt_ref, wvt_ref,
                       cos_ref, sa_ref, sb_ref, cosr_ref, sinr_ref,
                       x1_ref, z_ref, xbc_ref, dt_ref, qt_ref, k_ref, vt_ref, acc_ref):
    x1 = _swiglu_residual(x_ref[...], g1_ref, wg_ref, wu_ref, wd_ref, acc_ref)
    x1_ref[...] = x1
    h = _rms(x1, gmix_ref[...]).astype(BF16)
    cos, sa, sb = cos_ref[...], sa_ref[...], sb_ref[...]
    k = _dot(h, win_ref[:, IN_K:IN_TOTAL])
    for s in range(DIFF_WIDTH // LANES):
        sl = slice(s * LANES, (s + 1) * LANES)
        k_ref[:, sl] = _rope(k[:, sl], cos, sa, sb).astype(BF16)
    qt = _dot_nt(wqt_ref[...], h) * (DIFF_HEAD_DIM ** -0.5 * LOG2E)
    qt_ref[...] = _rope_rows(qt, cosr_ref[...], sinr_ref[...]).astype(BF16)
    vt = _dot_nt(wvt_ref[...], h).astype(BF16)
    ones = jnp.ones((BF16_ROWS, vt.shape[1]), BF16)
    for hd in range(DIFF_HEADS):
        vt_ref[hd * VT_ROWS:hd * VT_ROWS + DIFF_V_DIM, :] = vt[hd * DIFF_V_DIM:(hd + 1) * DIFF_V_DIM]
        vt_ref[hd * VT_ROWS + DIFF_V_DIM:(hd + 1) * VT_ROWS, :] = ones
    hp = _permute_chunks(_chunk_perm(False), h)
    dt_ref[...] = _dot(hp, win_ref[:, IN_DT:IN_K])
    z_ref[...] = _dot(hp, win_ref[:, IN_Z:IN_XBC])
    xbc_ref[...] = _dot(hp, win_ref[:, IN_XBC:IN_DT])


def _ffn_inproj_call(xf, g1, wg, wu, wd, gmix, win, wqt, wvt, cos_t, sa_t, sb_t, cos_r, sin_r, seq, tm):
    t, d = xf.shape
    nseq = seq // tm
    row = lambda w: pl.BlockSpec((tm, w), lambda i: (i, 0))
    colb = lambda rows: pl.BlockSpec((rows, tm), lambda i: (0, i))
    tab = pl.BlockSpec((tm, LANES), lambda i: (i % nseq, 0))
    tabr = pl.BlockSpec((ROPE_DIM // 2, tm), lambda i: (0, i % nseq))
    return pl.pallas_call(
        _ffn_inproj_kernel,
        grid=(t // tm,),
        in_specs=[row(d), _const_spec((1, d)),
                  _const_spec(wg.shape), _const_spec(wu.shape), _const_spec(wd.shape),
                  _const_spec((1, d)), _const_spec(win.shape), _const_spec(wqt.shape),
                  _const_spec(wvt.shape), tab, tab, tab, tabr, tabr],
        out_specs=[row(d), row(SSD_WIDTH), row(SSD_CONV_DIM), row(DT_PAD),
                   colb(DIFF_WIDTH), row(DIFF_WIDTH), colb(DIFF_HEADS * VT_ROWS)],
        out_shape=[jax.ShapeDtypeStruct((t, d), F32),
                   jax.ShapeDtypeStruct((t, SSD_WIDTH), F32),
                   jax.ShapeDtypeStruct((t, SSD_CONV_DIM), F32),
                   jax.ShapeDtypeStruct((t, DT_PAD), F32),
                   jax.ShapeDtypeStruct((DIFF_WIDTH, t), BF16),
                   jax.ShapeDtypeStruct((t, DIFF_WIDTH), BF16),
                   jax.ShapeDtypeStruct((DIFF_HEADS * VT_ROWS, t), BF16)],
        scratch_shapes=[pltpu.VMEM((tm, d), F32)],
        compiler_params=pltpu.CompilerParams(dimension_semantics=("arbitrary",),
                                             vmem_limit_bytes=VMEM_LIMIT),
        name="ffn_inproj",
    )(xf, g1, wg, wu, wd, gmix, win, wqt, wvt, cos_t, sa_t, sb_t, cos_r, sin_r)


def _pair_expand(tile, c0):
    rows = tile.shape[0]
    lane = lax.broadcasted_iota(jnp.int32, (rows, LANES), 1)
    lo = jnp.broadcast_to(tile[:, c0:c0 + 1], (rows, LANES))
    hi = jnp.broadcast_to(tile[:, c0 + 1:c0 + 2], (rows, LANES))
    return jnp.where(lane < SSD_HEAD_DIM, lo, hi)


def _chunk_time(r):
    nph = SSD_CHUNK // SUBLANES
    return jnp.right_shift(r, SUBLANES.bit_length() - 1) + (r & (SUBLANES - 1)) * nph


def _chunk_perm(transpose):
    row = lax.broadcasted_iota(jnp.int32, (SSD_CHUNK, SSD_CHUNK), 0)
    col = lax.broadcasted_iota(jnp.int32, (SSD_CHUNK, SSD_CHUNK), 1)
    hit = (row == _chunk_time(col)) if transpose else (col == _chunk_time(row))
    return jnp.where(hit, 1.0, 0.0).astype(BF16)


def _permute_chunks(perm, v):
    return jnp.concatenate(
        [_dot(perm, v[c:c + SSD_CHUNK]).astype(BF16) for c in range(0, v.shape[0], SSD_CHUNK)], axis=0)


def _chunk_cumsum(a, d):
    nph = SSD_CHUNK // SUBLANES
    slabs = [a[p * SUBLANES:(p + 1) * SUBLANES] for p in range(nph)]
    part = [None] * nph
    run = None
    for p in (range(nph) if d == 0 else reversed(range(nph))):
        run = slabs[p] if run is None else run + slabs[p]
        part[p] = run
    sub = lax.broadcasted_iota(jnp.int32, run.shape, 0)
    incl = run
    shift = 1
    while shift < SUBLANES:
        if d == 0:
            incl = incl + jnp.where(sub >= shift, pltpu.roll(incl, shift, axis=0), 0.0)
        else:
            incl = incl + jnp.where(sub < SUBLANES - shift, pltpu.roll(incl, SUBLANES - shift, axis=0), 0.0)
        shift *= 2
    off = incl - run
    acum = jnp.concatenate([part[p] + off for p in range(nph)], axis=0)
    tot = incl[SUBLANES - 1:SUBLANES] if d == 0 else incl[0:1]
    return acum, tot


def _ssd_decay_terms(dtraw, dtbias, a_neg, d):
    v = dtraw + dtbias
    dt = jnp.maximum(v, 0.0) + jnp.log(1.0 + jnp.exp(-jnp.abs(v)))
    acum, tot = _chunk_cumsum(dt * (a_neg * LOG2E), d)
    return dict(acum=acum, acum_t=acum.T, dt_t=dt.T, eac=jnp.exp2(acum),
                wl=dt * jnp.exp2(tot - acum), cdec=jnp.exp2(tot))


def _ssd_chunk(xc, terms, state_ref, d, with_skip, dexp):
    L = SSD_CHUNK
    row = _chunk_time(lax.broadcasted_iota(jnp.int32, (L, L), 0))
    col = _chunk_time(lax.broadcasted_iota(jnp.int32, (L, L), 1))
    xs = xc[:, :SSD_WIDTH]
    valid = row >= col if d == 0 else row <= col
    acum, acum_t, dt_t = terms["acum"], terms["acum_t"], terms["dt_t"]
    eac, wl, cdec = terms["eac"], terms["wl"], terms["cdec"]
    lane2 = lax.broadcasted_iota(jnp.int32, (2 * L, LANES), 1)
    row2 = lax.broadcasted_iota(jnp.int32, (2 * L, LANES), 0)
    bd_mask = (jnp.right_shift(row2, L.bit_length() - 1)
               == jnp.right_shift(lane2, SSD_HEAD_DIM.bit_length() - 1))
    ys = []
    for g in range(SSD_GROUPS):
        bg = xc[:, SSD_WIDTH + g * SSD_STATE:SSD_WIDTH + (g + 1) * SSD_STATE]
        cg = xc[:, SSD_WIDTH + (SSD_GROUPS + g) * SSD_STATE:SSD_WIDTH + (SSD_GROUPS + g + 1) * SSD_STATE]
        bt = bg.T.astype(BF16)
        cb16 = cg.astype(BF16)
        cb = _dot(cb16, bt)
        prev = state_ref[d, g]
        yoff = _dot(cb16, prev.astype(BF16))
        w_parts, cd_parts = [], []
        for p in range(2):
            h0 = 4 * g + 2 * p
            c0 = SSD_HEADS * d + h0
            ms = []
            for c in (c0, c0 + 1):
                diff = acum[:, c:c + 1] - acum_t[c:c + 1, :]
                dec = jnp.exp2(jnp.where(valid, diff, -jnp.inf))
                ms.append((cb * dec * dt_t[c:c + 1, :]).astype(BF16))
            lhs = jnp.concatenate(ms, axis=1)
            xp = xs[:, h0 * SSD_HEAD_DIM:(h0 + 2) * SSD_HEAD_DIM]
            xp2 = jnp.concatenate([xp, xp], axis=0)
            rhs = jnp.where(bd_mask, xp2, 0.0).astype(BF16)
            ydiag = _dot(lhs, rhs)
            y = ydiag + yoff[:, p * LANES:(p + 1) * LANES] * _pair_expand(eac, c0)
            if with_skip:
                y = y + dexp[:, h0 * SSD_HEAD_DIM:(h0 + 2) * SSD_HEAD_DIM] * xp
            ys.append(y)
            w_parts.append((xp * _pair_expand(wl, c0)).astype(BF16))
            cd_parts.append(_pair_expand(cdec, c0))
        w = jnp.concatenate(w_parts, axis=1)
        cd = jnp.concatenate(cd_parts, axis=1)
        state_ref[d, g] = prev * cd + _dot(bt, w)
    return jnp.concatenate(ys, axis=1)


def _ssd_kernel(xf_ref, xfp_ref, xfn_ref, xb_ref, xbp_ref, xbn_ref, dtf_ref, dtb_ref,
                cw_ref, cbias_ref, dtbias_ref, alog_ref, dexp_ref,
                yf_ref, yb_ref, state_ref, *, rc):
    j = pl.program_id(1)
    nb = pl.num_programs(1)
    nph = SSD_CHUNK // SUBLANES
    pad = SSD_CONV // 2
    halo = pad * SUBLANES

    @pl.when(j == 0)
    def _():
        state_ref[...] = jnp.zeros_like(state_ref)

    a_neg = -jnp.exp(alog_ref[...])
    dtbias = dtbias_ref[...]
    dexp = dexp_ref[...]
    cdim = cw_ref.shape[1]
    taps = [jnp.broadcast_to(cw_ref[k:k + 1, :], (SUBLANES, cdim)) for k in range(SSD_CONV)]
    bias = jnp.broadcast_to(cbias_ref[...], (SUBLANES, cdim))
    sub = lax.broadcasted_iota(jnp.int32, (SUBLANES, cdim), 0)

    def conv_chunk(x_ref, prev_ref, next_ref, blk, c):
        lo = c * SSD_CHUNK
        slab = lambda p: x_ref[lo + p * SUBLANES:lo + (p + 1) * SUBLANES, :]
        if c > 0:
            before = x_ref[lo - halo:lo, :]
        else:
            before = jnp.where(blk > 0, prev_ref[...], 0.0)
        if c < rc - 1:
            after = x_ref[lo + SSD_CHUNK:lo + SSD_CHUNK + halo, :]
        else:
            after = jnp.where(blk < nb - 1, next_ref[...], 0.0)
        window = [None] * (nph + 2 * pad)
        for p in range(nph):
            window[p + pad] = slab(p)
        for q in range(pad):
            src = nph - pad + q
            prev_slab = before[q * SUBLANES:(q + 1) * SUBLANES, :]
            window[q] = jnp.where(sub == 0, pltpu.roll(prev_slab, 1, axis=0), pltpu.roll(slab(src), 1, axis=0))
            next_slab = after[q * SUBLANES:(q + 1) * SUBLANES, :]
            window[nph + pad + q] = jnp.where(sub == SUBLANES - 1,
                                              pltpu.roll(next_slab, SUBLANES - 1, axis=0),
                                              pltpu.roll(slab(q), SUBLANES - 1, axis=0))
        out = []
        for p in range(nph):
            acc = bias
            for k in range(SSD_CONV):
                acc = acc + window[p + k] * taps[k]
            out.append(_silu(acc))
        return jnp.concatenate(out, axis=0)

    rows_of = lambda c: slice(c * SSD_CHUNK, (c + 1) * SSD_CHUNK)
    terms_f = [_ssd_decay_terms(dtf_ref[rows_of(c), :], dtbias, a_neg, 0) for c in range(rc)]
    terms_b = [_ssd_decay_terms(dtb_ref[rows_of(c), :], dtbias, a_neg, 1) for c in range(rc)]
    xc_f = [conv_chunk(xf_ref, xfp_ref, xfn_ref, j, c) for c in range(rc)]
    xc_b = [conv_chunk(xb_ref, xbp_ref, xbn_ref, nb - 1 - j, c) for c in range(rc)]
    for c in range(rc):
        cr = rc - 1 - c
        yf_ref[rows_of(c), :] = _ssd_chunk(xc_f[c], terms_f[c], state_ref, 0, True, dexp)
        yb_ref[rows_of(cr), :] = _ssd_chunk(xc_b[cr], terms_b[cr], state_ref, 1, False, dexp)


def _ssd_call(xbc, dt, cw, cbias, dtbias, alog, dexp, rc):
    b, s, cdim = xbc.shape
    rows = rc * SSD_CHUNK
    nb = s // rows
    halo_rows = (SSD_CONV // 2) * SUBLANES
    hb = rows // halo_rows
    nh = s // halo_rows
    main = lambda w, rev: pl.BlockSpec(
        (None, rows, w), (lambda i, j: (i, nb - 1 - j, 0)) if rev else (lambda i, j: (i, j, 0)))

    def halo(rev, nxt):
        def idx(i, j):
            blk = nb - 1 - j if rev else j
            h = (blk + 1) * hb if nxt else blk * hb - 1
            return (i, jnp.clip(h, 0, nh - 1), 0)
        return pl.BlockSpec((None, halo_rows, cdim), idx)

    return pl.pallas_call(
        functools.partial(_ssd_kernel, rc=rc),
        grid=(b, nb),
        in_specs=[main(cdim, False), halo(False, False), halo(False, True),
                  main(cdim, True), halo(True, False), halo(True, True),
                  main(DT_PAD, False), main(DT_PAD, True),
                  _const_spec(cw.shape), _const_spec(cbias.shape), _const_spec(dtbias.shape),
                  _const_spec(alog.shape), _const_spec(dexp.shape)],
        out_specs=[main(SSD_WIDTH, False), main(SSD_WIDTH, True)],
        out_shape=[jax.ShapeDtypeStruct((b, s, SSD_WIDTH), F32),
                   jax.ShapeDtypeStruct((b, s, SSD_WIDTH), F32)],
        scratch_shapes=[pltpu.VMEM((2, SSD_GROUPS, SSD_STATE, 4 * SSD_HEAD_DIM), F32)],
        compiler_params=pltpu.CompilerParams(dimension_semantics=("arbitrary", "arbitrary"),
                                             vmem_limit_bytes=VMEM_LIMIT),
        name="ssd",
    )(xbc, xbc, xbc, xbc, xbc, xbc, dt, dt, cw, cbias, dtbias, alog, dexp)


def _diffattn_kernel(qt_ref, k_ref, vt_ref, lam_ref, g_ref, o_ref, s_ref, a_ref, m_ref,
                     *, lambda_init, tq, hps):
    seq = k_ref.shape[0]
    lamv = lam_ref[...]
    l1 = jnp.sum(lamv[0:1] * lamv[1:2], axis=-1, keepdims=True)
    l2 = jnp.sum(lamv[2:3] * lamv[3:4], axis=-1, keepdims=True)
    lam = jnp.exp(l1) - jnp.exp(l2) + lambda_init
    gcol = g_ref[...] * (1.0 - lambda_init)
    nq = seq // tq
    nblk = hps * nq
    zero_half = jnp.zeros((DIFF_HEAD_DIM, tq), BF16)

    def locate(i):
        head = i // nq
        return head, pl.multiple_of((i - head * nq) * tq, tq)

    def scores(i, slot):
        head, off = locate(i)
        hoff = pl.multiple_of(head * DIFF_V_DIM, DIFF_V_DIM)
        qt = qt_ref[pl.ds(hoff, DIFF_V_DIM), pl.ds(off, tq)]
        k = k_ref[:, pl.ds(hoff, DIFF_V_DIM)]
        halves = [jnp.concatenate([qt[:DIFF_HEAD_DIM], zero_half], axis=0),
                  jnp.concatenate([zero_half, qt[DIFF_HEAD_DIM:]], axis=0)]
        for t in range(2):
            s = _dot(k, halves[t])
            s_ref[slot, t] = s
            m_ref[slot, t] = jnp.max(s, axis=0, keepdims=True)

    def softmax(slot):
        for t in range(2):
            a_ref[slot, t] = jnp.exp2(s_ref[slot, t] - m_ref[slot, t]).astype(BF16)

    def values(i, slot):
        head, off = locate(i)
        vt = vt_ref[pl.ds(pl.multiple_of(head * VT_ROWS, BF16_ROWS), VT_ROWS), :]
        halves = []
        for t in range(2):
            oa = _dot(vt, a_ref[slot, t])
            halves.append(oa[:DIFF_V_DIM] * (1.0 / oa[DIFF_V_DIM:DIFF_V_DIM + 1]))
        o = halves[0] - lam * halves[1]
        ms = jnp.sum(o * o, axis=0, keepdims=True) * (1.0 / DIFF_V_DIM)
        y = o * lax.rsqrt(ms + NORM_EPS) * gcol
        o_ref[pl.ds(off, tq), pl.ds(pl.multiple_of(head * DIFF_V_DIM, DIFF_V_DIM), DIFF_V_DIM)] = (
            y.T.astype(o_ref.dtype))

    scores(0, 0)
    scores(1, 1)
    softmax(0)

    def pair(j, carry):
        i = 2 * j + 1
        scores(i + 1, 0)
        values(i - 1, 0)
        softmax(1)
        scores(i + 2, 1)
        values(i, 1)
        softmax(0)
        return carry

    lax.fori_loop(0, (nblk - 2) // 2, pair, 0)
    values(nblk - 2, 0)
    softmax(1)
    values(nblk - 1, 1)


def _diffattn_call(qt, k, vt, lamv, gcol, lambda_init, seq, tq, hps):
    t = k.shape[0]
    assert (hps * seq // tq) % 2 == 0 and DIFF_HEADS % hps == 0
    col = lambda rows: pl.BlockSpec((hps * rows, seq), lambda i, h: (h, i))
    row = pl.BlockSpec((seq, hps * DIFF_V_DIM), lambda i, h: (i, h))
    return pl.pallas_call(
        functools.partial(_diffattn_kernel, lambda_init=lambda_init, tq=tq, hps=hps),
        grid=(t // seq, DIFF_HEADS // hps),
        in_specs=[col(DIFF_V_DIM), row, col(VT_ROWS), _const_spec(lamv.shape), _const_spec(gcol.shape)],
        out_specs=row,
        out_shape=jax.ShapeDtypeStruct((t, DIFF_WIDTH), BF16),
        scratch_shapes=[pltpu.VMEM((2, 2, seq, tq), F32),
                        pltpu.VMEM((2, 2, seq, tq), BF16),
                        pltpu.VMEM((2, 2, 1, tq), F32)],
        compiler_params=pltpu.CompilerParams(dimension_semantics=("arbitrary", "arbitrary"),
                                             vmem_limit_bytes=VMEM_LIMIT),
        name="diffattn",
    )(qt, k, vt, lamv, gcol)


def _tail_kernel(x1_ref, yf_ref, yb_ref, z_ref, yd_ref, kt_ref, vm_ref,
                 gssd_ref, wout_ref, gx_ref, wq_ref, wo_ref,
                 g2_ref, wg_ref, wu_ref, wd_ref, gfin_ref,
                 o_ref, acc_ref):
    x1d = x1_ref[...] + _dot(yd_ref[...], wout_ref[SSD_WIDTH:, :])
    y = (yf_ref[...] + yb_ref[...]) * _silu(z_ref[...])
    gw = SSD_WIDTH // SSD_GROUPS
    gssd = gssd_ref[...]
    yn = jnp.concatenate(
        [_rms(y[:, g * gw:(g + 1) * gw], gssd[:, g * gw:(g + 1) * gw]) for g in range(SSD_GROUPS)],
        axis=1).astype(BF16)
    yn = _permute_chunks(_chunk_perm(True), yn)
    x2 = x1d + _dot(yn, wout_ref[0:SSD_WIDTH, :])
    h = _rms(x2, gx_ref[...]).astype(BF16)
    qx = (_dot(h, wq_ref[...]) * (XATTN_HEAD_DIM ** -0.5)).astype(BF16)
    heads = []
    for hd in range(XATTN_HEADS):
        sl = slice(hd * XATTN_HEAD_DIM, (hd + 1) * XATTN_HEAD_DIM)
        s = _dot(qx[:, sl], kt_ref[sl, :])
        p = jnp.exp(s - jnp.max(s, axis=-1, keepdims=True))
        p = p * (1.0 / jnp.sum(p, axis=-1, keepdims=True))
        heads.append(_dot(p.astype(BF16), vm_ref[:, sl]).astype(BF16))
    x3 = x2 + _dot(jnp.concatenate(heads, axis=1), wo_ref[...])
    x4 = _swiglu_residual(x3, g2_ref, wg_ref, wu_ref, wd_ref, acc_ref)
    o_ref[...] = _rms(x4, gfin_ref[...])


def _tail_call(x1, yf, yb, z, yd, kt, vm, gssd, wout, gx, wq, wo, g2, wg, wu, wd, gfin, seq, tm):
    t, d = x1.shape
    nseq = seq // tm
    m = vm.shape[1]
    row = lambda w: pl.BlockSpec((tm, w), lambda i: (i, 0))
    return pl.pallas_call(
        _tail_kernel,
        grid=(t // tm,),
        in_specs=[row(d), row(SSD_WIDTH), row(SSD_WIDTH), row(SSD_WIDTH), row(DIFF_WIDTH),
                  pl.BlockSpec((None, d, m), lambda i: (i // nseq, 0, 0)),
                  pl.BlockSpec((None, m, d), lambda i: (i // nseq, 0, 0)),
                  _const_spec(gssd.shape), _const_spec(wout.shape), _const_spec(gx.shape),
                  _const_spec(wq.shape), _const_spec(wo.shape), _const_spec(g2.shape),
                  _const_spec(wg.shape), _const_spec(wu.shape), _const_spec(wd.shape),
                  _const_spec(gfin.shape)],
        out_specs=row(d),
        out_shape=jax.ShapeDtypeStruct((t, d), F32),
        scratch_shapes=[pltpu.VMEM((tm, d), F32)],
        compiler_params=pltpu.CompilerParams(dimension_semantics=("arbitrary",),
                                             vmem_limit_bytes=VMEM_LIMIT),
        name="tail",
    )(x1, yf, yb, z, yd, kt, vm, gssd, wout, gx, wq, wo, g2, wg, wu, wd, gfin)


def _rope_lane_tables(seq):
    pos = jnp.arange(seq, dtype=F32)
    inv_freq = 1.0 / (ROPE_THETA ** (jnp.arange(0, ROPE_DIM, 2, dtype=F32) / ROPE_DIM))
    ang = pos[:, None] * inv_freq[None, :]
    cos, sin = jnp.cos(ang), jnp.sin(ang)
    half = ROPE_DIM // 2
    pad = jnp.zeros((seq, DIFF_HEAD_DIM - ROPE_DIM), F32)
    zeros = jnp.zeros((seq, half), F32)
    cos64 = jnp.concatenate([cos, cos, pad + 1.0], axis=1)
    sa64 = jnp.concatenate([zeros, sin, pad], axis=1)
    sb64 = jnp.concatenate([-sin, zeros, pad], axis=1)
    tile = lambda a: jnp.concatenate([a, a], axis=1)
    return tile(cos64), tile(sa64), tile(sb64), cos.T, sin.T


def _ff_chunks(w_gate, w_up, w_down):
    return w_gate.astype(BF16), w_up.astype(BF16), w_down.astype(BF16)


def _tiling(seq):
    tm = min(512, seq)
    tq = 256
    hps = 2
    rc = min(4, seq // SSD_CHUNK)
    assert seq % tm == 0 and seq % tq == 0 and seq % (rc * SSD_CHUNK) == 0
    return tm, tq, hps, rc


def _pad_lanes(v, fill):
    return jnp.concatenate([v.astype(F32), jnp.full((LANES - v.shape[0],), fill, F32)])[None, :]


def kernel(x, mem, ffn1_norm_g, ffn1_w_gate, ffn1_w_up, ffn1_w_down, mix_norm_g, w_in, conv_w, conv_b, dt_bias_fwd, dt_bias_bwd, A_log_fwd, A_log_bwd, D_skip, ssd_norm_g, lambda_q1, lambda_k1, lambda_q2, lambda_k2, diff_subln_g, w_out, xattn_norm_g, mem_norm_g, xattn_w_q, xattn_w_kv, xattn_w_o, ffn2_norm_g, ffn2_w_gate, ffn2_w_up, ffn2_w_down, final_norm_g):
    b, s, d = x.shape
    depth = w_in.shape[0]
    assert depth == 1, "single-layer block only"
    tm, tq, hps, rc = _tiling(s)
    cos_t, sa_t, sb_t, cos_r, sin_r = _rope_lane_tables(s)
    xf = x.reshape(b * s, d)
    row = lambda v: v.astype(F32)[None, :]
    for layer in range(depth):
        lambda_init = 0.8 - 0.6 * math.exp(-0.3 * layer)
        w_kv = xattn_w_kv[layer]
        kt, vm = _kv_call(mem, row(mem_norm_g[layer]),
                          w_kv[:, :d].T.astype(BF16), w_kv[:, d:].astype(BF16))
        wg1, wu1, wd1 = _ff_chunks(ffn1_w_gate[layer], ffn1_w_up[layer], ffn1_w_down[layer])
        n_ssd_in = SSD_WIDTH + SSD_CONV_DIM + 2 * SSD_HEADS
        wl = w_in[layer]
        w_q = wl[:, n_ssd_in:n_ssd_in + DIFF_WIDTH]
        w_k = wl[:, n_ssd_in + DIFF_WIDTH:n_ssd_in + 2 * DIFF_WIDTH]
        w_v = wl[:, n_ssd_in + 2 * DIFF_WIDTH:]
        win = jnp.concatenate(
            [wl[:, :n_ssd_in], jnp.zeros((d, DT_PAD - 2 * SSD_HEADS), wl.dtype), w_k],
            axis=1).astype(BF16)
        x1, z, xbc, dt, qt, k, vt = _ffn_inproj_call(
            xf, row(ffn1_norm_g[layer]), wg1, wu1, wd1, row(mix_norm_g[layer]), win,
            w_q.T.astype(BF16), w_v.T.astype(BF16), cos_t, sa_t, sb_t, cos_r, sin_r, s, tm)
        cw = jnp.concatenate([conv_w[layer].astype(F32),
                              jnp.zeros((SUBLANES - SSD_CONV, SSD_CONV_DIM), F32)], axis=0)
        dtbias = _pad_lanes(jnp.concatenate([dt_bias_fwd[layer], dt_bias_bwd[layer]]), 0.0)
        alog = _pad_lanes(jnp.concatenate([A_log_fwd[layer], A_log_bwd[layer]]), -1e4)
        dexp = jnp.repeat(D_skip[layer].astype(F32), SSD_HEAD_DIM)[None, :]
        yf, yb = _ssd_call(xbc.reshape(b, s, SSD_CONV_DIM), dt.reshape(b, s, DT_PAD),
                           cw, row(conv_b[layer]), dtbias, alog, dexp, rc)
        lamv = jnp.stack([lambda_q1[layer], lambda_k1[layer],
                          lambda_q2[layer], lambda_k2[layer]]).astype(F32)
        yd = _diffattn_call(qt, k, vt, lamv, diff_subln_g[layer].astype(F32)[:, None],
                            lambda_init, s, tq, hps)
        wg2, wu2, wd2 = _ff_chunks(ffn2_w_gate[layer], ffn2_w_up[layer], ffn2_w_down[layer])
        gfin = row(final_norm_g)
        xf = _tail_call(x1, yf.reshape(b * s, SSD_WIDTH), yb.reshape(b * s, SSD_WIDTH), z,
                        yd, kt, vm,
                        row(ssd_norm_g[layer]), w_out[layer].astype(BF16), row(xattn_norm_g[layer]),
                        xattn_w_q[layer].astype(BF16), xattn_w_o[layer].astype(BF16),
                        row(ffn2_norm_g[layer]), wg2, wu2, wd2, gfin, s, tm)
    return xf.reshape(b, s, d)
```

```python
import functools
import math

import jax
import jax.numpy as jnp
from jax import lax
from jax.experimental import pallas as pl
from jax.experimental.pallas import tpu as pltpu

F32 = jnp.float32
BF16 = jnp.bfloat16

NORM_EPS = 1e-5
LOG2E = math.log2(math.e)
D_MODEL = 1024
D_FF = 2816
FF_CHUNK = 256
N_FF_CHUNKS = D_FF // FF_CHUNK
SSD_WIDTH = 512
SSD_HEADS = 8
SSD_HEAD_DIM = 64
SSD_GROUPS = 2
SSD_GROUP_HEADS = SSD_HEADS // SSD_GROUPS
SSD_STATE = 128
SSD_CONV = 5
SSD_CHUNK = 128
SSD_CONV_DIM = 1024
DIFF_WIDTH = 512
DIFF_HEADS = 4
DIFF_HEAD_DIM = 64
DIFF_V_DIM = 128
ROPE_THETA = 500000.0
ROPE_DIM = 16
XATTN_HEADS = 4
XATTN_HEAD_DIM = 256
LANES = 128
SUBLANES = 8
BF16_ROWS = 2 * SUBLANES
VT_ROWS = DIFF_V_DIM + BF16_ROWS
DT_PAD = LANES
IN_Z, IN_XBC, IN_DT = 0, SSD_WIDTH, SSD_WIDTH + SSD_CONV_DIM
IN_K = IN_DT + DT_PAD
IN_TOTAL = IN_K + DIFF_WIDTH
VMEM_LIMIT = 56 * 1024 * 1024


def _dot(a, b):
    return jnp.dot(a, b, preferred_element_type=F32)


def _rms(xf, g):
    ms = jnp.sum(xf * xf, axis=-1, keepdims=True) * (1.0 / xf.shape[-1])
    return xf * lax.rsqrt(ms + NORM_EPS) * g


def _silu(x):
    return x * jax.nn.sigmoid(x)


def _swiglu_residual(x, g_ref, wg_ref, wu_ref, wd_ref, acc_ref):
    h = _rms(x, g_ref[...]).astype(BF16)
    acc_ref[...] = jnp.zeros_like(acc_ref)
    for c in range(N_FF_CHUNKS):
        cols = slice(c * FF_CHUNK, (c + 1) * FF_CHUNK)
        gate = _dot(h, wg_ref[:, cols])
        up = _dot(h, wu_ref[:, cols])
        act = (_silu(gate) * up).astype(BF16)
        acc_ref[...] += _dot(act, wd_ref[cols, :])
    return x + 0.5 * acc_ref[...]


def _const_spec(shape):
    nd = len(shape)
    return pl.BlockSpec(shape, lambda *_: (0,) * nd, pipeline_mode=pl.Buffered(1))


def _kv_kernel(mem_ref, g_ref, wkt_ref, wv_ref, kt_ref, v_ref):
    m = _rms(mem_ref[...], g_ref[...]).astype(BF16)
    kt = lax.dot_general(wkt_ref[...], m, (((1,), (1,)), ((), ())),
                         preferred_element_type=F32)
    kt_ref[...] = kt.astype(BF16)
    v_ref[...] = _dot(m, wv_ref[...]).astype(BF16)


def _kv_call(mem, g, wkt, wv):
    b, m, d = mem.shape
    return pl.pallas_call(
        _kv_kernel,
        grid=(b,),
        in_specs=[pl.BlockSpec((None, m, d), lambda i: (i, 0, 0)),
                  _const_spec((1, d)), _const_spec((d, d)), _const_spec((d, d))],
        out_specs=[pl.BlockSpec((None, d, m), lambda i: (i, 0, 0)),
                   pl.BlockSpec((None, m, d), lambda i: (i, 0, 0))],
        out_shape=[jax.ShapeDtypeStruct((b, d, m), BF16), jax.ShapeDtypeStruct((b, m, d), BF16)],
        compiler_params=pltpu.CompilerParams(dimension_semantics=("arbitrary",),
                                             vmem_limit_bytes=VMEM_LIMIT),
        name="kv",
    )(mem, g, wkt, wv)


def _rope(t, cos, sa, sb):
    return t * cos + pltpu.roll(t, 8, axis=1) * sa + pltpu.roll(t, LANES - 8, axis=1) * sb


def _dot_nt(a, b):
    return lax.dot_general(a, b, (((1,), (1,)), ((), ())), preferred_element_type=F32)


def _rope_rows(tt, cos, sin):
    half = ROPE_DIM // 2
    pieces = []
    for base in range(0, DIFF_WIDTH, DIFF_HEAD_DIM):
        t1 = tt[base:base + half]
        t2 = tt[base + half:base + ROPE_DIM]
        pieces += [t1 * cos - t2 * sin, t2 * cos + t1 * sin, tt[base + ROPE_DIM:base + DIFF_HEAD_DIM]]
    return jnp.concatenate(pieces, axis=0)


def _ffn_inproj_kernel(x_ref, g1_ref, wg_ref, wu_ref, wd_ref, gmix_ref, win_ref, wqt_ref, wvt_ref,
                       cos_ref, sa_ref, sb_ref, cosr_ref, sinr_ref,
                       x1_ref, z_ref, xbc_ref, dt_ref, qt_ref, k_ref, vt_ref, acc_ref):
    x1 = _swiglu_residual(x_ref[...], g1_ref, wg_ref, wu_ref, wd_ref, acc_ref)
    x1_ref[...] = x1
    h = _rms(x1, gmix_ref[...]).astype(BF16)
    cos, sa, sb = cos_ref[...], sa_ref[...], sb_ref[...]
    k = _dot(h, win_ref[:, IN_K:IN_TOTAL])
    for s in range(DIFF_WIDTH // LANES):
        sl = slice(s * LANES, (s + 1) * LANES)
        k_ref[:, sl] = _rope(k[:, sl], cos, sa, sb).astype(BF16)
    qt = _dot_nt(wqt_ref[...], h) * (DIFF_HEAD_DIM ** -0.5 * LOG2E)
    qt_ref[...] = _rope_rows(qt, cosr_ref[...], sinr_ref[...]).astype(BF16)
    vt = _dot_nt(wvt_ref[...], h).astype(BF16)
    ones = jnp.ones((BF16_ROWS, vt.shape[1]), BF16)
    for hd in range(DIFF_HEADS):
        vt_ref[hd * VT_ROWS:hd * VT_ROWS + DIFF_V_DIM, :] = vt[hd * DIFF_V_DIM:(hd + 1) * DIFF_V_DIM]
        vt_ref[hd * VT_ROWS + DIFF_V_DIM:(hd + 1) * VT_ROWS, :] = ones
    hp = _permute_chunks(_chunk_perm(False), h)
    dt_ref[...] = _dot(hp, win_ref[:, IN_DT:IN_K])
    z_ref[...] = _dot(hp, win_ref[:, IN_Z:IN_XBC])
    xbc_ref[...] = _dot(hp, win_ref[:, IN_XBC:IN_DT])


def _ffn_inproj_call(xf, g1, wg, wu, wd, gmix, win, wqt, wvt, cos_t, sa_t, sb_t, cos_r, sin_r, seq, tm):
    t, d = xf.shape
    nseq = seq // tm
    row = lambda w: pl.BlockSpec((tm, w), lambda i: (i, 0))
    colb = lambda rows: pl.BlockSpec((rows, tm), lambda i: (0, i))
    tab = pl.BlockSpec((tm, LANES), lambda i: (i % nseq, 0))
    tabr = pl.BlockSpec((ROPE_DIM // 2, tm), lambda i: (0, i % nseq))
    return pl.pallas_call(
        _ffn_inproj_kernel,
        grid=(t // tm,),
        in_specs=[row(d), _const_spec((1, d)),
                  _const_spec(wg.shape), _const_spec(wu.shape), _const_spec(wd.shape),
                  _const_spec((1, d)), _const_spec(win.shape), _const_spec(wqt.shape),
                  _const_spec(wvt.shape), tab, tab, tab, tabr, tabr],
        out_specs=[row(d), row(SSD_WIDTH), row(SSD_CONV_DIM), row(DT_PAD),
                   colb(DIFF_WIDTH), row(DIFF_WIDTH), colb(DIFF_HEADS * VT_ROWS)],
        out_shape=[jax.ShapeDtypeStruct((t, d), F32),
                   jax.ShapeDtypeStruct((t, SSD_WIDTH), F32),
                   jax.ShapeDtypeStruct((t, SSD_CONV_DIM), F32),
                   jax.ShapeDtypeStruct((t, DT_PAD), F32),
                   jax.ShapeDtypeStruct((DIFF_WIDTH, t), BF16),
                   jax.ShapeDtypeStruct((t, DIFF_WIDTH), BF16),
                   jax.ShapeDtypeStruct((DIFF_HEADS * VT_ROWS, t), BF16)],
        scratch_shapes=[pltpu.VMEM((tm, d), F32)],
        compiler_params=pltpu.CompilerParams(dimension_semantics=("arbitrary",),
                                             vmem_limit_bytes=VMEM_LIMIT),
        name="ffn_inproj",
    )(xf, g1, wg, wu, wd, gmix, win, wqt, wvt, cos_t, sa_t, sb_t, cos_r, sin_r)


def _pair_expand(tile, c0):
    rows = tile.shape[0]
    lane = lax.broadcasted_iota(jnp.int32, (rows, LANES), 1)
    lo = jnp.broadcast_to(tile[:, c0:c0 + 1], (rows, LANES))
    hi = jnp.broadcast_to(tile[:, c0 + 1:c0 + 2], (rows, LANES))
    return jnp.where(lane < SSD_HEAD_DIM, lo, hi)


def _chunk_time(r):
    nph = SSD_CHUNK // SUBLANES
    return jnp.right_shift(r, SUBLANES.bit_length() - 1) + (r & (SUBLANES - 1)) * nph


def _chunk_perm(transpose):
    row = lax.broadcasted_iota(jnp.int32, (SSD_CHUNK, SSD_CHUNK), 0)
    col = lax.broadcasted_iota(jnp.int32, (SSD_CHUNK, SSD_CHUNK), 1)
    hit = (row == _chunk_time(col)) if transpose else (col == _chunk_time(row))
    return jnp.where(hit, 1.0, 0.0).astype(BF16)


def _permute_chunks(perm, v):
    return jnp.concatenate(
        [_dot(perm, v[c:c + SSD_CHUNK]).astype(BF16) for c in range(0, v.shape[0], SSD_CHUNK)], axis=0)


def _chunk_cumsum(a, d):
    nph = SSD_CHUNK // SUBLANES
    slabs = [a[p * SUBLANES:(p + 1) * SUBLANES] for p in range(nph)]
    part = [None] * nph
    run = None
    for p in (range(nph) if d == 0 else reversed(range(nph))):
        run = slabs[p] if run is None else run + slabs[p]
        part[p] = run
    sub = lax.broadcasted_iota(jnp.int32, run.shape, 0)
    incl = run
    shift = 1
    while shift < SUBLANES:
        if d == 0:
            incl = incl + jnp.where(sub >= shift, pltpu.roll(incl, shift, axis=0), 0.0)
        else:
            incl = incl + jnp.where(sub < SUBLANES - shift, pltpu.roll(incl, SUBLANES - shift, axis=0), 0.0)
        shift *= 2
    off = incl - run
    acum = jnp.concatenate([part[p] + off for p in range(nph)], axis=0)
    tot = incl[SUBLANES - 1:SUBLANES] if d == 0 else incl[0:1]
    return acum, tot


def _ssd_decay_terms(dtraw, dtbias, a_neg, d):
    v = dtraw + dtbias
    dt = jnp.maximum(v, 0.0) + jnp.log(1.0 + jnp.exp(-jnp.abs(v)))
    acum, tot = _chunk_cumsum(dt * (a_neg * LOG2E), d)
    return dict(acum=acum, acum_t=acum.T, dt_t=dt.T, eac=jnp.exp2(acum),
                wl=dt * jnp.exp2(tot - acum), cdec=jnp.exp2(tot))


def _ssd_chunk(xc, terms, state_ref, d, with_skip, dexp):
    L = SSD_CHUNK
    row = _chunk_time(lax.broadcasted_iota(jnp.int32, (L, L), 0))
    col = _chunk_time(lax.broadcasted_iota(jnp.int32, (L, L), 1))
    xs = xc[:, :SSD_WIDTH]
    valid = row >= col if d == 0 else row <= col
    acum, acum_t, dt_t = terms["acum"], terms["acum_t"], terms["dt_t"]
    eac, wl, cdec = terms["eac"], terms["wl"], terms["cdec"]
    lane2 = lax.broadcasted_iota(jnp.int32, (2 * L, LANES), 1)
    row2 = lax.broadcasted_iota(jnp.int32, (2 * L, LANES), 0)
    bd_mask = (jnp.right_shift(row2, L.bit_length() - 1)
               == jnp.right_shift(lane2, SSD_HEAD_DIM.bit_length() - 1))
    ys = []
    for g in range(SSD_GROUPS):
        bg = xc[:, SSD_WIDTH + g * SSD_STATE:SSD_WIDTH + (g + 1) * SSD_STATE]
        cg = xc[:, SSD_WIDTH + (SSD_GROUPS + g) * SSD_STATE:SSD_WIDTH + (SSD_GROUPS + g + 1) * SSD_STATE]
        bt = bg.T.astype(BF16)
        cb16 = cg.astype(BF16)
        cb = _dot(cb16, bt)
        prev = state_ref[d, g]
        yoff = _dot(cb16, prev.astype(BF16))
        w_parts, cd_parts = [], []
        for p in range(SSD_GROUP_HEADS // 2):
            h0 = SSD_GROUP_HEADS * g + 2 * p
            c0 = SSD_HEADS * d + h0
            ms = []
            for c in (c0, c0 + 1):
                diff = acum[:, c:c + 1] - acum_t[c:c + 1, :]
                dec = jnp.exp2(jnp.where(valid, diff, -jnp.inf))
                ms.append((cb * dec * dt_t[c:c + 1, :]).astype(BF16))
            lhs = jnp.concatenate(ms, axis=1)
            xp = xs[:, h0 * SSD_HEAD_DIM:(h0 + 2) * SSD_HEAD_DIM]
            xp2 = jnp.concatenate([xp, xp], axis=0)
            rhs = jnp.where(bd_mask, xp2, 0.0).astype(BF16)
            ydiag = _dot(lhs, rhs)
            y = ydiag + yoff[:, p * LANES:(p + 1) * LANES] * _pair_expand(eac, c0)
            if with_skip:
                y = y + dexp[:, h0 * SSD_HEAD_DIM:(h0 + 2) * SSD_HEAD_DIM] * xp
            ys.append(y)
            w_parts.append((xp * _pair_expand(wl, c0)).astype(BF16))
            cd_parts.append(_pair_expand(cdec, c0))
        w = jnp.concatenate(w_parts, axis=1)
        cd = jnp.concatenate(cd_parts, axis=1)
        state_ref[d, g] = prev * cd + _dot(bt, w)
    return jnp.concatenate(ys, axis=1)


def _ssd_kernel(xf_ref, xfp_ref, xfn_ref, xb_ref, xbp_ref, xbn_ref, dtf_ref, dtb_ref,
                cw_ref, cbias_ref, dtbias_ref, alog_ref, dexp_ref,
                yf_ref, yb_ref, state_ref, *, rc):
    j = pl.program_id(1)
    nb = pl.num_programs(1)
    nph = SSD_CHUNK // SUBLANES
    pad = SSD_CONV // 2
    halo = pad * SUBLANES

    @pl.when(j == 0)
    def _():
        state_ref[...] = jnp.zeros_like(state_ref)

    a_neg = -jnp.exp(alog_ref[...])
    dtbias = dtbias_ref[...]
    dexp = dexp_ref[...]
    cdim = cw_ref.shape[1]
    taps = [jnp.broadcast_to(cw_ref[k:k + 1, :], (SUBLANES, cdim)) for k in range(SSD_CONV)]
    bias = jnp.broadcast_to(cbias_ref[...], (SUBLANES, cdim))
    sub = lax.broadcasted_iota(jnp.int32, (SUBLANES, cdim), 0)

    def conv_chunk(x_ref, prev_ref, next_ref, blk, c):
        lo = c * SSD_CHUNK
        slab = lambda p: x_ref[lo + p * SUBLANES:lo + (p + 1) * SUBLANES, :]
        if c > 0:
            before = x_ref[lo - halo:lo, :]
        else:
            before = jnp.where(blk > 0, prev_ref[...], 0.0)
        if c < rc - 1:
            after = x_ref[lo + SSD_CHUNK:lo + SSD_CHUNK + halo, :]
        else:
            after = jnp.where(blk < nb - 1, next_ref[...], 0.0)
        window = [None] * (nph + 2 * pad)
        for p in range(nph):
            window[p + pad] = slab(p)
        for q in range(pad):
            src = nph - pad + q
            prev_slab = before[q * SUBLANES:(q + 1) * SUBLANES, :]
            window[q] = jnp.where(sub == 0, pltpu.roll(prev_slab, 1, axis=0), pltpu.roll(slab(src), 1, axis=0))
            next_slab = after[q * SUBLANES:(q + 1) * SUBLANES, :]
            window[nph + pad + q] = jnp.where(sub == SUBLANES - 1,
                                              pltpu.roll(next_slab, SUBLANES - 1, axis=0),
                                              pltpu.roll(slab(q), SUBLANES - 1, axis=0))
        out = []
        for p in range(nph):
            acc = bias
            for k in range(SSD_CONV):
                acc = acc + window[p + k] * taps[k]
            out.append(_silu(acc))
        return jnp.concatenate(out, axis=0)

    rows_of = lambda c: slice(c * SSD_CHUNK, (c + 1) * SSD_CHUNK)
    terms_f = [_ssd_decay_terms(dtf_ref[rows_of(c), :], dtbias, a_neg, 0) for c in range(rc)]
    terms_b = [_ssd_decay_terms(dtb_ref[rows_of(c), :], dtbias, a_neg, 1) for c in range(rc)]
    xc_f = [conv_chunk(xf_ref, xfp_ref, xfn_ref, j, c) for c in range(rc)]
    xc_b = [conv_chunk(xb_ref, xbp_ref, xbn_ref, nb - 1 - j, c) for c in range(rc)]
    for c in range(rc):
        cr = rc - 1 - c
        yf_ref[rows_of(c), :] = _ssd_chunk(xc_f[c], terms_f[c], state_ref, 0, True, dexp)
        yb_ref[rows_of(cr), :] = _ssd_chunk(xc_b[cr], terms_b[cr], state_ref, 1, False, dexp)


def _ssd_call(xbc, dt, cw, cbias, dtbias, alog, dexp, rc):
    b, s, cdim = xbc.shape
    rows = rc * SSD_CHUNK
    nb = s // rows
    halo_rows = (SSD_CONV // 2) * SUBLANES
    hb = rows // halo_rows
    nh = s // halo_rows
    main = lambda w, rev: pl.BlockSpec(
        (None, rows, w), (lambda i, j: (i, nb - 1 - j, 0)) if rev else (lambda i, j: (i, j, 0)))

    def halo(rev, nxt):
        def idx(i, j):
            blk = nb - 1 - j if rev else j
            h = (blk + 1) * hb if nxt else blk * hb - 1
            return (i, jnp.clip(h, 0, nh - 1), 0)
        return pl.BlockSpec((None, halo_rows, cdim), idx)

    return pl.pallas_call(
        functools.partial(_ssd_kernel, rc=rc),
        grid=(b, nb),
        in_specs=[main(cdim, False), halo(False, False), halo(False, True),
                  main(cdim, True), halo(True, False), halo(True, True),
                  main(DT_PAD, False), main(DT_PAD, True),
                  _const_spec(cw.shape), _const_spec(cbias.shape), _const_spec(dtbias.shape),
                  _const_spec(alog.shape), _const_spec(dexp.shape)],
        out_specs=[main(SSD_WIDTH, False), main(SSD_WIDTH, True)],
        out_shape=[jax.ShapeDtypeStruct((b, s, SSD_WIDTH), F32),
                   jax.ShapeDtypeStruct((b, s, SSD_WIDTH), F32)],
        scratch_shapes=[pltpu.VMEM((2, SSD_GROUPS, SSD_STATE, SSD_GROUP_HEADS * SSD_HEAD_DIM), F32)],
        compiler_params=pltpu.CompilerParams(dimension_semantics=("arbitrary", "arbitrary"),
                                             vmem_limit_bytes=VMEM_LIMIT),
        name="ssd",
    )(xbc, xbc, xbc, xbc, xbc, xbc, dt, dt, cw, cbias, dtbias, alog, dexp)


def _diffattn_kernel(qt_ref, k_ref, vt_ref, lam_ref, g_ref, o_ref, s_ref, a_ref, m_ref,
                     *, lambda_init, tq, hps):
    seq = k_ref.shape[0]
    lamv = lam_ref[...]
    l1 = jnp.sum(lamv[0:1] * lamv[1:2], axis=-1, keepdims=True)
    l2 = jnp.sum(lamv[2:3] * lamv[3:4], axis=-1, keepdims=True)
    lam = jnp.exp(l1) - jnp.exp(l2) + lambda_init
    gcol = g_ref[...] * (1.0 - lambda_init)
    nq = seq // tq
    nblk = hps * nq
    zero_half = jnp.zeros((DIFF_HEAD_DIM, tq), BF16)

    def locate(i):
        head = i // nq
        return head, pl.multiple_of((i - head * nq) * tq, tq)

    def scores(i, slot):
        head, off = locate(i)
        hoff = pl.multiple_of(head * DIFF_V_DIM, DIFF_V_DIM)
        qt = qt_ref[pl.ds(hoff, DIFF_V_DIM), pl.ds(off, tq)]
        k = k_ref[:, pl.ds(hoff, DIFF_V_DIM)]
        halves = [jnp.concatenate([qt[:DIFF_HEAD_DIM], zero_half], axis=0),
                  jnp.concatenate([zero_half, qt[DIFF_HEAD_DIM:]], axis=0)]
        for t in range(2):
            s = _dot(k, halves[t])
            s_ref[slot, t] = s
            m_ref[slot, t] = jnp.max(s, axis=0, keepdims=True)

    def softmax(slot):
        for t in range(2):
            a_ref[slot, t] = jnp.exp2(s_ref[slot, t] - m_ref[slot, t]).astype(BF16)

    def values(i, slot):
        head, off = locate(i)
        vt = vt_ref[pl.ds(pl.multiple_of(head * VT_ROWS, BF16_ROWS), VT_ROWS), :]
        halves = []
        for t in range(2):
            oa = _dot(vt, a_ref[slot, t])
            halves.append(oa[:DIFF_V_DIM] * (1.0 / oa[DIFF_V_DIM:DIFF_V_DIM + 1]))
        o = halves[0] - lam * halves[1]
        ms = jnp.sum(o * o, axis=0, keepdims=True) * (1.0 / DIFF_V_DIM)
        y = o * lax.rsqrt(ms + NORM_EPS) * gcol
        o_ref[pl.ds(off, tq), pl.ds(pl.multiple_of(head * DIFF_V_DIM, DIFF_V_DIM), DIFF_V_DIM)] = (
            y.T.astype(o_ref.dtype))

    scores(0, 0)
    scores(1, 1)
    softmax(0)

    def pair(j, carry):
        i = 2 * j + 1
        scores(i + 1, 0)
        values(i - 1, 0)
        softmax(1)
        scores(i + 2, 1)
        values(i, 1)
        softmax(0)
        return carry

    lax.fori_loop(0, (nblk - 2) // 2, pair, 0)
    values(nblk - 2, 0)
    softmax(1)
    values(nblk - 1, 1)


def _diffattn_call(qt, k, vt, lamv, gcol, lambda_init, seq, tq, hps):
    t = k.shape[0]
    assert (hps * seq // tq) % 2 == 0 and DIFF_HEADS % hps == 0
    col = lambda rows: pl.BlockSpec((hps * rows, seq), lambda i, h: (h, i))
    row = pl.BlockSpec((seq, hps * DIFF_V_DIM), lambda i, h: (i, h))
    return pl.pallas_call(
        functools.partial(_diffattn_kernel, lambda_init=lambda_init, tq=tq, hps=hps),
        grid=(t // seq, DIFF_HEADS // hps),
        in_specs=[col(DIFF_V_DIM), row, col(VT_ROWS), _const_spec(lamv.shape), _const_spec(gcol.shape)],
        out_specs=row,
        out_shape=jax.ShapeDtypeStruct((t, DIFF_WIDTH), BF16),
        scratch_shapes=[pltpu.VMEM((2, 2, seq, tq), F32),
                        pltpu.VMEM((2, 2, seq, tq), BF16),
                        pltpu.VMEM((2, 2, 1, tq), F32)],
        compiler_params=pltpu.CompilerParams(dimension_semantics=("arbitrary", "arbitrary"),
                                             vmem_limit_bytes=VMEM_LIMIT),
        name="diffattn",
    )(qt, k, vt, lamv, gcol)


def _tail_kernel(x1_ref, yf_ref, yb_ref, z_ref, yd_ref, kt_ref, vm_ref,
                 gssd_ref, wout_ref, gx_ref, wq_ref, wo_ref,
                 g2_ref, wg_ref, wu_ref, wd_ref, gfin_ref,
                 o_ref, acc_ref):
    x1d = x1_ref[...] + _dot(yd_ref[...], wout_ref[SSD_WIDTH:, :])
    y = (yf_ref[...] + yb_ref[...]) * _silu(z_ref[...])
    gw = SSD_WIDTH // SSD_GROUPS
    gssd = gssd_ref[...]
    yn = jnp.concatenate(
        [_rms(y[:, g * gw:(g + 1) * gw], gssd[:, g * gw:(g + 1) * gw]) for g in range(SSD_GROUPS)],
        axis=1).astype(BF16)
    yn = _permute_chunks(_chunk_perm(True), yn)
    x2 = x1d + _dot(yn, wout_ref[0:SSD_WIDTH, :])
    h = _rms(x2, gx_ref[...]).astype(BF16)
    qx = (_dot(h, wq_ref[...]) * (XATTN_HEAD_DIM ** -0.5)).astype(BF16)
    cols = [slice(hd * XATTN_HEAD_DIM, (hd + 1) * XATTN_HEAD_DIM) for hd in range(XATTN_HEADS)]
    scores = [_dot(qx[:, sl], kt_ref[sl, :]) for sl in cols]
    probs = []
    for s in scores:
        p = jnp.exp(s - jnp.max(s, axis=-1, keepdims=True))
        probs.append((p * (1.0 / jnp.sum(p, axis=-1, keepdims=True))).astype(BF16))
    heads = [_dot(p, vm_ref[:, sl]).astype(BF16) for p, sl in zip(probs, cols)]
    x3 = x2 + _dot(jnp.concatenate(heads, axis=1), wo_ref[...])
    x4 = _swiglu_residual(x3, g2_ref, wg_ref, wu_ref, wd_ref, acc_ref)
    o_ref[...] = _rms(x4, gfin_ref[...])


def _tail_call(x1, yf, yb, z, yd, kt, vm, gssd, wout, gx, wq, wo, g2, wg, wu, wd, gfin, seq, tm):
    t, d = x1.shape
    nseq = seq // tm
    m = vm.shape[1]
    row = lambda w: pl.BlockSpec((tm, w), lambda i: (i, 0))
    return pl.pallas_call(
        _tail_kernel,
        grid=(t // tm,),
        in_specs=[row(d), row(SSD_WIDTH), row(SSD_WIDTH), row(SSD_WIDTH), row(DIFF_WIDTH),
                  pl.BlockSpec((None, d, m), lambda i: (i // nseq, 0, 0)),
                  pl.BlockSpec((None, m, d), lambda i: (i // nseq, 0, 0)),
                  _const_spec(gssd.shape), _const_spec(wout.shape), _const_spec(gx.shape),
                  _const_spec(wq.shape), _const_spec(wo.shape), _const_spec(g2.shape),
                  _const_spec(wg.shape), _const_spec(wu.shape), _const_spec(wd.shape),
                  _const_spec(gfin.shape)],
        out_specs=row(d),
        out_shape=jax.ShapeDtypeStruct((t, d), F32),
        scratch_shapes=[pltpu.VMEM((tm, d), F32)],
        compiler_params=pltpu.CompilerParams(dimension_semantics=("arbitrary",),
                                             vmem_limit_bytes=VMEM_LIMIT),
        name="tail",
    )(x1, yf, yb, z, yd, kt, vm, gssd, wout, gx, wq, wo, g2, wg, wu, wd, gfin)


def _rope_lane_tables(seq):
    pos = jnp.arange(seq, dtype=F32)
    inv_freq = 1.0 / (ROPE_THETA ** (jnp.arange(0, ROPE_DIM, 2, dtype=F32) / ROPE_DIM))
    ang = pos[:, None] * inv_freq[None, :]
    cos, sin = jnp.cos(ang), jnp.sin(ang)
    half = ROPE_DIM // 2
    pad = jnp.zeros((seq, DIFF_HEAD_DIM - ROPE_DIM), F32)
    zeros = jnp.zeros((seq, half), F32)
    cos64 = jnp.concatenate([cos, cos, pad + 1.0], axis=1)
    sa64 = jnp.concatenate([zeros, sin, pad], axis=1)
    sb64 = jnp.concatenate([-sin, zeros, pad], axis=1)
    tile = lambda a: jnp.concatenate([a, a], axis=1)
    return tile(cos64), tile(sa64), tile(sb64), cos.T, sin.T


def _ff_chunks(w_gate, w_up, w_down):
    return w_gate.astype(BF16), w_up.astype(BF16), w_down.astype(BF16)


def _tiling(seq):
    tm = min(512, seq)
    tq = 256
    hps = 2
    rc = min(4, seq // SSD_CHUNK)
    assert seq % tm == 0 and seq % tq == 0 and seq % (rc * SSD_CHUNK) == 0
    return tm, tq, hps, rc


def _pad_lanes(v, fill):
    return jnp.concatenate([v.astype(F32), jnp.full((LANES - v.shape[0],), fill, F32)])[None, :]


def kernel(x, mem, ffn1_norm_g, ffn1_w_gate, ffn1_w_up, ffn1_w_down, mix_norm_g, w_in, conv_w, conv_b, dt_bias_fwd, dt_bias_bwd, A_log_fwd, A_log_bwd, D_skip, ssd_norm_g, lambda_q1, lambda_k1, lambda_q2, lambda_k2, diff_subln_g, w_out, xattn_norm_g, mem_norm_g, xattn_w_q, xattn_w_kv, xattn_w_o, ffn2_norm_g, ffn2_w_gate, ffn2_w_up, ffn2_w_down, final_norm_g):
    b, s, d = x.shape
    depth = w_in.shape[0]
    assert depth == 1, "single-layer block only"
    tm, tq, hps, rc = _tiling(s)
    cos_t, sa_t, sb_t, cos_r, sin_r = _rope_lane_tables(s)
    xf = x.reshape(b * s, d)
    row = lambda v: v.astype(F32)[None, :]
    for layer in range(depth):
        lambda_init = 0.8 - 0.6 * math.exp(-0.3 * layer)
        w_kv = xattn_w_kv[layer]
        kt, vm = _kv_call(mem, row(mem_norm_g[layer]),
                          w_kv[:, :d].T.astype(BF16), w_kv[:, d:].astype(BF16))
        wg1, wu1, wd1 = _ff_chunks(ffn1_w_gate[layer], ffn1_w_up[layer], ffn1_w_down[layer])
        n_ssd_in = SSD_WIDTH + SSD_CONV_DIM + 2 * SSD_HEADS
        wl = w_in[layer]
        w_q = wl[:, n_ssd_in:n_ssd_in + DIFF_WIDTH]
        w_k = wl[:, n_ssd_in + DIFF_WIDTH:n_ssd_in + 2 * DIFF_WIDTH]
        w_v = wl[:, n_ssd_in + 2 * DIFF_WIDTH:]
        win = jnp.concatenate(
            [wl[:, :n_ssd_in], jnp.zeros((d, DT_PAD - 2 * SSD_HEADS), wl.dtype), w_k],
            axis=1).astype(BF16)
        x1, z, xbc, dt, qt, k, vt = _ffn_inproj_call(
            xf, row(ffn1_norm_g[layer]), wg1, wu1, wd1, row(mix_norm_g[layer]), win,
            w_q.T.astype(BF16), w_v.T.astype(BF16), cos_t, sa_t, sb_t, cos_r, sin_r, s, tm)
        cw = jnp.concatenate([conv_w[layer].astype(F32),
                              jnp.zeros((SUBLANES - SSD_CONV, SSD_CONV_DIM), F32)], axis=0)
        dtbias = _pad_lanes(jnp.concatenate([dt_bias_fwd[layer], dt_bias_bwd[layer]]), 0.0)
        alog = _pad_lanes(jnp.concatenate([A_log_fwd[layer], A_log_bwd[layer]]), -1e4)
        dexp = jnp.repeat(D_skip[layer].astype(F32), SSD_HEAD_DIM)[None, :]
        yf, yb = _ssd_call(xbc.reshape(b, s, SSD_CONV_DIM), dt.reshape(b, s, DT_PAD),
                           cw, row(conv_b[layer]), dtbias, alog, dexp, rc)
        lamv = jnp.stack([lambda_q1[layer], lambda_k1[layer],
                          lambda_q2[layer], lambda_k2[layer]]).astype(F32)
        yd = _diffattn_call(qt, k, vt, lamv, diff_subln_g[layer].astype(F32)[:, None],
                            lambda_init, s, tq, hps)
        wg2, wu2, wd2 = _ff_chunks(ffn2_w_gate[layer], ffn2_w_up[layer], ffn2_w_down[layer])
        gfin = row(final_norm_g)
        xf = _tail_call(x1, yf.reshape(b * s, SSD_WIDTH), yb.reshape(b * s, SSD_WIDTH), z,
                        yd, kt, vm,
                        row(ssd_norm_g[layer]), w_out[layer].astype(BF16), row(xattn_norm_g[layer]),
                        xattn_w_q[layer].astype(BF16), xattn_w_o[layer].astype(BF16),
                        row(ffn2_norm_g[layer]), wg2, wu2, wd2, gfin, s, tm)
    return xf.reshape(b, s, d)
```

```python
import functools
import math

import jax
import jax.numpy as jnp
from jax import lax
from jax.experimental import pallas as pl
from jax.experimental.pallas import tpu as pltpu

F32 = jnp.float32
BF16 = jnp.bfloat16

NORM_EPS = 1e-5
LOG2E = math.log2(math.e)
D_MODEL = 1024
D_FF = 2816
FF_CHUNK = 256
N_FF_CHUNKS = D_FF // FF_CHUNK
SSD_WIDTH = 512
SSD_HEADS = 8
SSD_HEAD_DIM = 64
SSD_GROUPS = 2
SSD_GROUP_HEADS = SSD_HEADS // SSD_GROUPS
SSD_STATE = 128
SSD_CONV = 5
SSD_CHUNK = 128
SSD_CONV_DIM = 1024
DIFF_WIDTH = 512
DIFF_HEADS = 4
DIFF_HEAD_DIM = 64
DIFF_V_DIM = 128
ROPE_THETA = 500000.0
ROPE_DIM = 16
XATTN_HEADS = 4
XATTN_HEAD_DIM = 256
LANES = 128
SUBLANES = 8
BF16_ROWS = 2 * SUBLANES
VT_ROWS = DIFF_V_DIM + BF16_ROWS
DT_PAD = LANES
IN_Z, IN_XBC, IN_DT = 0, SSD_WIDTH, SSD_WIDTH + SSD_CONV_DIM
IN_K = IN_DT + DT_PAD
IN_TOTAL = IN_K + DIFF_WIDTH
VMEM_LIMIT = 56 * 1024 * 1024
ATTN_VMEM_LIMIT = 60 * 1024 * 1024


def _dot(a, b):
    return jnp.dot(a, b, preferred_element_type=F32)


def _rms(xf, g):
    ms = jnp.sum(xf * xf, axis=-1, keepdims=True) * (1.0 / xf.shape[-1])
    return xf * lax.rsqrt(ms + NORM_EPS) * g


def _silu(x):
    return x * jax.nn.sigmoid(x)


def _swiglu_residual(x, g_ref, wg_ref, wu_ref, wd_ref, acc_ref):
    h = _rms(x, g_ref[...]).astype(BF16)
    acc_ref[...] = jnp.zeros_like(acc_ref)
    for c in range(N_FF_CHUNKS):
        cols = slice(c * FF_CHUNK, (c + 1) * FF_CHUNK)
        gate = _dot(h, wg_ref[:, cols])
        up = _dot(h, wu_ref[:, cols])
        act = (_silu(gate) * up).astype(BF16)
        acc_ref[...] += _dot(act, wd_ref[cols, :])
    return x + 0.5 * acc_ref[...]


def _const_spec(shape):
    nd = len(shape)
    return pl.BlockSpec(shape, lambda *_: (0,) * nd, pipeline_mode=pl.Buffered(1))


def _kv_kernel(mem_ref, g_ref, wkt_ref, wv_ref, kt_ref, v_ref):
    m = _rms(mem_ref[...], g_ref[...]).astype(BF16)
    kt = lax.dot_general(wkt_ref[...], m, (((1,), (1,)), ((), ())),
                         preferred_element_type=F32)
    kt_ref[...] = kt.astype(BF16)
    v_ref[...] = _dot(m, wv_ref[...]).astype(BF16)


def _kv_call(mem, g, wkt, wv):
    b, m, d = mem.shape
    return pl.pallas_call(
        _kv_kernel,
        grid=(b,),
        in_specs=[pl.BlockSpec((None, m, d), lambda i: (i, 0, 0)),
                  _const_spec((1, d)), _const_spec((d, d)), _const_spec((d, d))],
        out_specs=[pl.BlockSpec((None, d, m), lambda i: (i, 0, 0)),
                   pl.BlockSpec((None, m, d), lambda i: (i, 0, 0))],
        out_shape=[jax.ShapeDtypeStruct((b, d, m), BF16), jax.ShapeDtypeStruct((b, m, d), BF16)],
        compiler_params=pltpu.CompilerParams(dimension_semantics=("arbitrary",),
                                             vmem_limit_bytes=VMEM_LIMIT),
        name="kv",
    )(mem, g, wkt, wv)


def _rope(t, cos, sa, sb):
    return t * cos + pltpu.roll(t, 8, axis=1) * sa + pltpu.roll(t, LANES - 8, axis=1) * sb


def _dot_nt(a, b):
    return lax.dot_general(a, b, (((1,), (1,)), ((), ())), preferred_element_type=F32)


def _rope_rows(tt, cos, sin):
    half = ROPE_DIM // 2
    pieces = []
    for base in range(0, DIFF_WIDTH, DIFF_HEAD_DIM):
        t1 = tt[base:base + half]
        t2 = tt[base + half:base + ROPE_DIM]
        pieces += [t1 * cos - t2 * sin, t2 * cos + t1 * sin, tt[base + ROPE_DIM:base + DIFF_HEAD_DIM]]
    return jnp.concatenate(pieces, axis=0)


def _ffn_inproj_kernel(x_ref, g1_ref, wg_ref, wu_ref, wd_ref, gmix_ref, win_ref, wqt_ref, wvt_ref,
                       cos_ref, sa_ref, sb_ref, cosr_ref, sinr_ref,
                       x1_ref, z_ref, xbc_ref, dt_ref, qt_ref, k_ref, vt_ref, acc_ref):
    x1 = _swiglu_residual(x_ref[...], g1_ref, wg_ref, wu_ref, wd_ref, acc_ref)
    x1_ref[...] = x1
    h = _rms(x1, gmix_ref[...]).astype(BF16)
    cos, sa, sb = cos_ref[...], sa_ref[...], sb_ref[...]
    k = _dot(h, win_ref[:, IN_K:IN_TOTAL])
    for s in range(DIFF_WIDTH // LANES):
        sl = slice(s * LANES, (s + 1) * LANES)
        k_ref[:, sl] = _rope(k[:, sl], cos, sa, sb).astype(BF16)
    qt = _dot_nt(wqt_ref[...], h) * (DIFF_HEAD_DIM ** -0.5 * LOG2E)
    qt_ref[...] = _rope_rows(qt, cosr_ref[...], sinr_ref[...]).astype(BF16)
    vt = _dot_nt(wvt_ref[...], h).astype(BF16)
    ones = jnp.ones((BF16_ROWS, vt.shape[1]), BF16)
    for hd in range(DIFF_HEADS):
        vt_ref[hd * VT_ROWS:hd * VT_ROWS + DIFF_V_DIM, :] = vt[hd * DIFF_V_DIM:(hd + 1) * DIFF_V_DIM]
        vt_ref[hd * VT_ROWS + DIFF_V_DIM:(hd + 1) * VT_ROWS, :] = ones
    hp = _permute_chunks(_chunk_perm(False), h)
    dt_ref[...] = _dot(hp, win_ref[:, IN_DT:IN_K])
    z_ref[...] = _dot(hp, win_ref[:, IN_Z:IN_XBC])
    xbc_ref[...] = _dot(hp, win_ref[:, IN_XBC:IN_DT])


def _ffn_inproj_call(xf, g1, wg, wu, wd, gmix, win, wqt, wvt, cos_t, sa_t, sb_t, cos_r, sin_r, seq, tm):
    t, d = xf.shape
    nseq = seq // tm
    row = lambda w: pl.BlockSpec((tm, w), lambda i: (i, 0))
    colb = lambda rows: pl.BlockSpec((rows, tm), lambda i: (0, i))
    tab = pl.BlockSpec((tm, LANES), lambda i: (i % nseq, 0))
    tabr = pl.BlockSpec((ROPE_DIM // 2, tm), lambda i: (0, i % nseq))
    return pl.pallas_call(
        _ffn_inproj_kernel,
        grid=(t // tm,),
        in_specs=[row(d), _const_spec((1, d)),
                  _const_spec(wg.shape), _const_spec(wu.shape), _const_spec(wd.shape),
                  _const_spec((1, d)), _const_spec(win.shape), _const_spec(wqt.shape),
                  _const_spec(wvt.shape), tab, tab, tab, tabr, tabr],
        out_specs=[row(d), row(SSD_WIDTH), row(SSD_CONV_DIM), row(DT_PAD),
                   colb(DIFF_WIDTH), row(DIFF_WIDTH), colb(DIFF_HEADS * VT_ROWS)],
        out_shape=[jax.ShapeDtypeStruct((t, d), F32),
                   jax.ShapeDtypeStruct((t, SSD_WIDTH), F32),
                   jax.ShapeDtypeStruct((t, SSD_CONV_DIM), F32),
                   jax.ShapeDtypeStruct((t, DT_PAD), F32),
                   jax.ShapeDtypeStruct((DIFF_WIDTH, t), BF16),
                   jax.ShapeDtypeStruct((t, DIFF_WIDTH), BF16),
                   jax.ShapeDtypeStruct((DIFF_HEADS * VT_ROWS, t), BF16)],
        scratch_shapes=[pltpu.VMEM((tm, d), F32)],
        compiler_params=pltpu.CompilerParams(dimension_semantics=("arbitrary",),
                                             vmem_limit_bytes=VMEM_LIMIT),
        name="ffn_inproj",
    )(xf, g1, wg, wu, wd, gmix, win, wqt, wvt, cos_t, sa_t, sb_t, cos_r, sin_r)


def _pair_expand(tile, c0):
    rows = tile.shape[0]
    lane = lax.broadcasted_iota(jnp.int32, (rows, LANES), 1)
    lo = jnp.broadcast_to(tile[:, c0:c0 + 1], (rows, LANES))
    hi = jnp.broadcast_to(tile[:, c0 + 1:c0 + 2], (rows, LANES))
    return jnp.where(lane < SSD_HEAD_DIM, lo, hi)


def _chunk_time(r):
    nph = SSD_CHUNK // SUBLANES
    return jnp.right_shift(r, SUBLANES.bit_length() - 1) + (r & (SUBLANES - 1)) * nph


def _chunk_perm(transpose):
    row = lax.broadcasted_iota(jnp.int32, (SSD_CHUNK, SSD_CHUNK), 0)
    col = lax.broadcasted_iota(jnp.int32, (SSD_CHUNK, SSD_CHUNK), 1)
    hit = (row == _chunk_time(col)) if transpose else (col == _chunk_time(row))
    return jnp.where(hit, 1.0, 0.0).astype(BF16)


def _permute_chunks(perm, v):
    return jnp.concatenate(
        [_dot(perm, v[c:c + SSD_CHUNK]).astype(BF16) for c in range(0, v.shape[0], SSD_CHUNK)], axis=0)


def _chunk_cumsum(a, d):
    nph = SSD_CHUNK // SUBLANES
    slabs = [a[p * SUBLANES:(p + 1) * SUBLANES] for p in range(nph)]
    part = [None] * nph
    run = None
    for p in (range(nph) if d == 0 else reversed(range(nph))):
        run = slabs[p] if run is None else run + slabs[p]
        part[p] = run
    sub = lax.broadcasted_iota(jnp.int32, run.shape, 0)
    incl = run
    shift = 1
    while shift < SUBLANES:
        if d == 0:
            incl = incl + jnp.where(sub >= shift, pltpu.roll(incl, shift, axis=0), 0.0)
        else:
            incl = incl + jnp.where(sub < SUBLANES - shift, pltpu.roll(incl, SUBLANES - shift, axis=0), 0.0)
        shift *= 2
    off = incl - run
    acum = jnp.concatenate([part[p] + off for p in range(nph)], axis=0)
    tot = incl[SUBLANES - 1:SUBLANES] if d == 0 else incl[0:1]
    return acum, tot


def _ssd_decay_terms(dtraw, dtbias, a_neg, d):
    v = dtraw + dtbias
    dt = jnp.maximum(v, 0.0) + jnp.log(1.0 + jnp.exp(-jnp.abs(v)))
    acum, tot = _chunk_cumsum(dt * (a_neg * LOG2E), d)
    return dict(acum=acum, acum_t=acum.T, dt_t=dt.T, eac=jnp.exp2(acum),
                wl=dt * jnp.exp2(tot - acum), cdec=jnp.exp2(tot))


def _ssd_chunk(xc, terms, state_ref, d, with_skip, dexp):
    L = SSD_CHUNK
    row = _chunk_time(lax.broadcasted_iota(jnp.int32, (L, L), 0))
    col = _chunk_time(lax.broadcasted_iota(jnp.int32, (L, L), 1))
    xs = xc[:, :SSD_WIDTH]
    valid = row >= col if d == 0 else row <= col
    acum, acum_t, dt_t = terms["acum"], terms["acum_t"], terms["dt_t"]
    eac, wl, cdec = terms["eac"], terms["wl"], terms["cdec"]
    lane2 = lax.broadcasted_iota(jnp.int32, (2 * L, LANES), 1)
    row2 = lax.broadcasted_iota(jnp.int32, (2 * L, LANES), 0)
    bd_mask = (jnp.right_shift(row2, L.bit_length() - 1)
               == jnp.right_shift(lane2, SSD_HEAD_DIM.bit_length() - 1))
    ys = []
    for g in range(SSD_GROUPS):
        bg = xc[:, SSD_WIDTH + g * SSD_STATE:SSD_WIDTH + (g + 1) * SSD_STATE]
        cg = xc[:, SSD_WIDTH + (SSD_GROUPS + g) * SSD_STATE:SSD_WIDTH + (SSD_GROUPS + g + 1) * SSD_STATE]
        bt = bg.T.astype(BF16)
        cb16 = cg.astype(BF16)
        cb = _dot(cb16, bt)
        prev = state_ref[d, g]
        yoff = _dot(cb16, prev.astype(BF16))
        w_parts, cd_parts = [], []
        for p in range(SSD_GROUP_HEADS // 2):
            h0 = SSD_GROUP_HEADS * g + 2 * p
            c0 = SSD_HEADS * d + h0
            ms = []
            for c in (c0, c0 + 1):
                diff = acum[:, c:c + 1] - acum_t[c:c + 1, :]
                dec = jnp.exp2(jnp.where(valid, diff, -jnp.inf))
                ms.append((cb * dec * dt_t[c:c + 1, :]).astype(BF16))
            lhs = jnp.concatenate(ms, axis=1)
            xp = xs[:, h0 * SSD_HEAD_DIM:(h0 + 2) * SSD_HEAD_DIM]
            xp2 = jnp.concatenate([xp, xp], axis=0)
            rhs = jnp.where(bd_mask, xp2, 0.0).astype(BF16)
            ydiag = _dot(lhs, rhs)
            y = ydiag + yoff[:, p * LANES:(p + 1) * LANES] * _pair_expand(eac, c0)
            if with_skip:
                y = y + dexp[:, h0 * SSD_HEAD_DIM:(h0 + 2) * SSD_HEAD_DIM] * xp
            ys.append(y)
            w_parts.append((xp * _pair_expand(wl, c0)).astype(BF16))
            cd_parts.append(_pair_expand(cdec, c0))
        w = jnp.concatenate(w_parts, axis=1)
        cd = jnp.concatenate(cd_parts, axis=1)
        state_ref[d, g] = prev * cd + _dot(bt, w)
    return jnp.concatenate(ys, axis=1)


def _ssd_kernel(xf_ref, xfp_ref, xfn_ref, xb_ref, xbp_ref, xbn_ref, dtf_ref, dtb_ref,
                cw_ref, cbias_ref, dtbias_ref, alog_ref, dexp_ref,
                yf_ref, yb_ref, state_ref, *, rc):
    j = pl.program_id(1)
    nb = pl.num_programs(1)
    nph = SSD_CHUNK // SUBLANES
    pad = SSD_CONV // 2
    halo = pad * SUBLANES

    @pl.when(j == 0)
    def _():
        state_ref[...] = jnp.zeros_like(state_ref)

    a_neg = -jnp.exp(alog_ref[...])
    dtbias = dtbias_ref[...]
    dexp = dexp_ref[...]
    cdim = cw_ref.shape[1]
    taps = [jnp.broadcast_to(cw_ref[k:k + 1, :], (SUBLANES, cdim)) for k in range(SSD_CONV)]
    bias = jnp.broadcast_to(cbias_ref[...], (SUBLANES, cdim))
    sub = lax.broadcasted_iota(jnp.int32, (SUBLANES, cdim), 0)

    def conv_chunk(x_ref, prev_ref, next_ref, blk, c):
        lo = c * SSD_CHUNK
        slab = lambda p: x_ref[lo + p * SUBLANES:lo + (p + 1) * SUBLANES, :]
        if c > 0:
            before = x_ref[lo - halo:lo, :]
        else:
            before = jnp.where(blk > 0, prev_ref[...], 0.0)
        if c < rc - 1:
            after = x_ref[lo + SSD_CHUNK:lo + SSD_CHUNK + halo, :]
        else:
            after = jnp.where(blk < nb - 1, next_ref[...], 0.0)
        window = [None] * (nph + 2 * pad)
        for p in range(nph):
            window[p + pad] = slab(p)
        for q in range(pad):
            src = nph - pad + q
            prev_slab = before[q * SUBLANES:(q + 1) * SUBLANES, :]
            window[q] = jnp.where(sub == 0, pltpu.roll(prev_slab, 1, axis=0), pltpu.roll(slab(src), 1, axis=0))
            next_slab = after[q * SUBLANES:(q + 1) * SUBLANES, :]
            window[nph + pad + q] = jnp.where(sub == SUBLANES - 1,
                                              pltpu.roll(next_slab, SUBLANES - 1, axis=0),
                                              pltpu.roll(slab(q), SUBLANES - 1, axis=0))
        out = []
        for p in range(nph):
            acc = bias
            for k in range(SSD_CONV):
                acc = acc + window[p + k] * taps[k]
            out.append(_silu(acc))
        return jnp.concatenate(out, axis=0)

    rows_of = lambda c: slice(c * SSD_CHUNK, (c + 1) * SSD_CHUNK)
    terms_f = [_ssd_decay_terms(dtf_ref[rows_of(c), :], dtbias, a_neg, 0) for c in range(rc)]
    terms_b = [_ssd_decay_terms(dtb_ref[rows_of(c), :], dtbias, a_neg, 1) for c in range(rc)]
    xc_f = [conv_chunk(xf_ref, xfp_ref, xfn_ref, j, c) for c in range(rc)]
    xc_b = [conv_chunk(xb_ref, xbp_ref, xbn_ref, nb - 1 - j, c) for c in range(rc)]
    for c in range(rc):
        cr = rc - 1 - c
        yf_ref[rows_of(c), :] = _ssd_chunk(xc_f[c], terms_f[c], state_ref, 0, True, dexp)
        yb_ref[rows_of(cr), :] = _ssd_chunk(xc_b[cr], terms_b[cr], state_ref, 1, False, dexp)


def _ssd_call(xbc, dt, cw, cbias, dtbias, alog, dexp, rc):
    b, s, cdim = xbc.shape
    rows = rc * SSD_CHUNK
    nb = s // rows
    halo_rows = (SSD_CONV // 2) * SUBLANES
    hb = rows // halo_rows
    nh = s // halo_rows
    main = lambda w, rev: pl.BlockSpec(
        (None, rows, w), (lambda i, j: (i, nb - 1 - j, 0)) if rev else (lambda i, j: (i, j, 0)))

    def halo(rev, nxt):
        def idx(i, j):
            blk = nb - 1 - j if rev else j
            h = (blk + 1) * hb if nxt else blk * hb - 1
            return (i, jnp.clip(h, 0, nh - 1), 0)
        return pl.BlockSpec((None, halo_rows, cdim), idx)

    return pl.pallas_call(
        functools.partial(_ssd_kernel, rc=rc),
        grid=(b, nb),
        in_specs=[main(cdim, False), halo(False, False), halo(False, True),
                  main(cdim, True), halo(True, False), halo(True, True),
                  main(DT_PAD, False), main(DT_PAD, True),
                  _const_spec(cw.shape), _const_spec(cbias.shape), _const_spec(dtbias.shape),
                  _const_spec(alog.shape), _const_spec(dexp.shape)],
        out_specs=[main(SSD_WIDTH, False), main(SSD_WIDTH, True)],
        out_shape=[jax.ShapeDtypeStruct((b, s, SSD_WIDTH), F32),
                   jax.ShapeDtypeStruct((b, s, SSD_WIDTH), F32)],
        scratch_shapes=[pltpu.VMEM((2, SSD_GROUPS, SSD_STATE, SSD_GROUP_HEADS * SSD_HEAD_DIM), F32)],
        compiler_params=pltpu.CompilerParams(dimension_semantics=("arbitrary", "arbitrary"),
                                             vmem_limit_bytes=VMEM_LIMIT),
        name="ssd",
    )(xbc, xbc, xbc, xbc, xbc, xbc, dt, dt, cw, cbias, dtbias, alog, dexp)


def _diffattn_kernel(qt_ref, k_ref, vt_ref, lam_ref, g_ref, o_ref, s_ref, a_ref, m_ref,
                     *, lambda_init, tq, hps):
    seq = k_ref.shape[0]
    lamv = lam_ref[...]
    l1 = jnp.sum(lamv[0:1] * lamv[1:2], axis=-1, keepdims=True)
    l2 = jnp.sum(lamv[2:3] * lamv[3:4], axis=-1, keepdims=True)
    lam = jnp.exp(l1) - jnp.exp(l2) + lambda_init
    gcol = g_ref[...] * (1.0 - lambda_init)
    nq = seq // tq
    nblk = hps * nq
    zero_half = jnp.zeros((DIFF_HEAD_DIM, tq), BF16)

    def locate(i):
        head = i // nq
        return head, pl.multiple_of((i - head * nq) * tq, tq)

    def scores(i, slot):
        head, off = locate(i)
        hoff = pl.multiple_of(head * DIFF_V_DIM, DIFF_V_DIM)
        qt = qt_ref[pl.ds(hoff, DIFF_V_DIM), pl.ds(off, tq)]
        k = k_ref[:, pl.ds(hoff, DIFF_V_DIM)]
        halves = [jnp.concatenate([qt[:DIFF_HEAD_DIM], zero_half], axis=0),
                  jnp.concatenate([zero_half, qt[DIFF_HEAD_DIM:]], axis=0)]
        for t in range(2):
            s = _dot(k, halves[t])
            s_ref[slot, t] = s
            m_ref[slot, t] = jnp.max(s, axis=0, keepdims=True)

    def softmax(slot):
        for t in range(2):
            a_ref[slot, t] = jnp.exp2(s_ref[slot, t] - m_ref[slot, t]).astype(BF16)

    def values(i, slot):
        head, off = locate(i)
        vt = vt_ref[pl.ds(pl.multiple_of(head * VT_ROWS, BF16_ROWS), VT_ROWS), :]
        halves = []
        for t in range(2):
            oa = _dot(vt, a_ref[slot, t])
            halves.append(oa[:DIFF_V_DIM] * (1.0 / oa[DIFF_V_DIM:DIFF_V_DIM + 1]))
        o = halves[0] - lam * halves[1]
        ms = jnp.sum(o * o, axis=0, keepdims=True) * (1.0 / DIFF_V_DIM)
        y = o * lax.rsqrt(ms + NORM_EPS) * gcol
        o_ref[pl.ds(off, tq), pl.ds(pl.multiple_of(head * DIFF_V_DIM, DIFF_V_DIM), DIFF_V_DIM)] = (
            y.T.astype(o_ref.dtype))

    scores(0, 0)
    scores(1, 1)
    softmax(0)

    def pair(j, carry):
        i = 2 * j + 1
        scores(i + 1, 0)
        values(i - 1, 0)
        softmax(1)
        scores(i + 2, 1)
        values(i, 1)
        softmax(0)
        return carry

    lax.fori_loop(0, (nblk - 2) // 2, pair, 0)
    values(nblk - 2, 0)
    softmax(1)
    values(nblk - 1, 1)


def _diffattn_call(qt, k, vt, lamv, gcol, lambda_init, seq, tq, hps):
    t = k.shape[0]
    assert (hps * seq // tq) % 2 == 0 and DIFF_HEADS % hps == 0
    col = lambda rows: pl.BlockSpec((hps * rows, seq), lambda i, h: (h, i))
    row = lambda mode: pl.BlockSpec((seq, hps * DIFF_V_DIM), lambda i, h: (i, h), pipeline_mode=mode)
    return pl.pallas_call(
        functools.partial(_diffattn_kernel, lambda_init=lambda_init, tq=tq, hps=hps),
        grid=(t // seq, DIFF_HEADS // hps),
        in_specs=[col(DIFF_V_DIM), row(None), col(VT_ROWS), _const_spec(lamv.shape),
                  _const_spec(gcol.shape)],
        out_specs=row(pl.Buffered(1)),
        out_shape=jax.ShapeDtypeStruct((t, DIFF_WIDTH), BF16),
        scratch_shapes=[pltpu.VMEM((2, 2, seq, tq), F32),
                        pltpu.VMEM((2, 2, seq, tq), BF16),
                        pltpu.VMEM((2, 2, 1, tq), F32)],
        compiler_params=pltpu.CompilerParams(dimension_semantics=("arbitrary", "arbitrary"),
                                             vmem_limit_bytes=ATTN_VMEM_LIMIT),
        name="diffattn",
    )(qt, k, vt, lamv, gcol)


def _tail_kernel(x1_ref, yf_ref, yb_ref, z_ref, yd_ref, kt_ref, vm_ref,
                 gssd_ref, wout_ref, gx_ref, wq_ref, wo_ref,
                 g2_ref, wg_ref, wu_ref, wd_ref, gfin_ref,
                 o_ref, acc_ref):
    x1d = x1_ref[...] + _dot(yd_ref[...], wout_ref[SSD_WIDTH:, :])
    y = (yf_ref[...] + yb_ref[...]) * _silu(z_ref[...])
    gw = SSD_WIDTH // SSD_GROUPS
    gssd = gssd_ref[...]
    yn = jnp.concatenate(
        [_rms(y[:, g * gw:(g + 1) * gw], gssd[:, g * gw:(g + 1) * gw]) for g in range(SSD_GROUPS)],
        axis=1).astype(BF16)
    yn = _permute_chunks(_chunk_perm(True), yn)
    x2 = x1d + _dot(yn, wout_ref[0:SSD_WIDTH, :])
    h = _rms(x2, gx_ref[...]).astype(BF16)
    qx = (_dot(h, wq_ref[...]) * (XATTN_HEAD_DIM ** -0.5)).astype(BF16)
    cols = [slice(hd * XATTN_HEAD_DIM, (hd + 1) * XATTN_HEAD_DIM) for hd in range(XATTN_HEADS)]
    scores = [_dot(qx[:, sl], kt_ref[sl, :]) for sl in cols]
    probs = []
    for s in scores:
        p = jnp.exp(s - jnp.max(s, axis=-1, keepdims=True))
        probs.append((p * (1.0 / jnp.sum(p, axis=-1, keepdims=True))).astype(BF16))
    heads = [_dot(p, vm_ref[:, sl]).astype(BF16) for p, sl in zip(probs, cols)]
    x3 = x2 + _dot(jnp.concatenate(heads, axis=1), wo_ref[...])
    x4 = _swiglu_residual(x3, g2_ref, wg_ref, wu_ref, wd_ref, acc_ref)
    o_ref[...] = _rms(x4, gfin_ref[...])


def _tail_call(x1, yf, yb, z, yd, kt, vm, gssd, wout, gx, wq, wo, g2, wg, wu, wd, gfin, seq, tm):
    t, d = x1.shape
    nseq = seq // tm
    m = vm.shape[1]
    row = lambda w: pl.BlockSpec((tm, w), lambda i: (i, 0))
    return pl.pallas_call(
        _tail_kernel,
        grid=(t // tm,),
        in_specs=[row(d), row(SSD_WIDTH), row(SSD_WIDTH), row(SSD_WIDTH), row(DIFF_WIDTH),
                  pl.BlockSpec((None, d, m), lambda i: (i // nseq, 0, 0)),
                  pl.BlockSpec((None, m, d), lambda i: (i // nseq, 0, 0)),
                  _const_spec(gssd.shape), _const_spec(wout.shape), _const_spec(gx.shape),
                  _const_spec(wq.shape), _const_spec(wo.shape), _const_spec(g2.shape),
                  _const_spec(wg.shape), _const_spec(wu.shape), _const_spec(wd.shape),
                  _const_spec(gfin.shape)],
        out_specs=row(d),
        out_shape=jax.ShapeDtypeStruct((t, d), F32),
        scratch_shapes=[pltpu.VMEM((tm, d), F32)],
        compiler_params=pltpu.CompilerParams(dimension_semantics=("arbitrary",),
                                             vmem_limit_bytes=VMEM_LIMIT),
        name="tail",
    )(x1, yf, yb, z, yd, kt, vm, gssd, wout, gx, wq, wo, g2, wg, wu, wd, gfin)


def _rope_lane_tables(seq):
    pos = jnp.arange(seq, dtype=F32)
    inv_freq = 1.0 / (ROPE_THETA ** (jnp.arange(0, ROPE_DIM, 2, dtype=F32) / ROPE_DIM))
    ang = pos[:, None] * inv_freq[None, :]
    cos, sin = jnp.cos(ang), jnp.sin(ang)
    half = ROPE_DIM // 2
    pad = jnp.zeros((seq, DIFF_HEAD_DIM - ROPE_DIM), F32)
    zeros = jnp.zeros((seq, half), F32)
    cos64 = jnp.concatenate([cos, cos, pad + 1.0], axis=1)
    sa64 = jnp.concatenate([zeros, sin, pad], axis=1)
    sb64 = jnp.concatenate([-sin, zeros, pad], axis=1)
    tile = lambda a: jnp.concatenate([a, a], axis=1)
    return tile(cos64), tile(sa64), tile(sb64), cos.T, sin.T


def _ff_chunks(w_gate, w_up, w_down):
    return w_gate.astype(BF16), w_up.astype(BF16), w_down.astype(BF16)


def _tiling(seq):
    tm = min(512, seq)
    tq = 256
    hps = DIFF_HEADS
    rc = min(4, seq // SSD_CHUNK)
    assert seq % tm == 0 and seq % tq == 0 and seq % (rc * SSD_CHUNK) == 0
    return tm, tq, hps, rc


def _pad_lanes(v, fill):
    return jnp.concatenate([v.astype(F32), jnp.full((LANES - v.shape[0],), fill, F32)])[None, :]


def kernel(x, mem, ffn1_norm_g, ffn1_w_gate, ffn1_w_up, ffn1_w_down, mix_norm_g, w_in, conv_w, conv_b, dt_bias_fwd, dt_bias_bwd, A_log_fwd, A_log_bwd, D_skip, ssd_norm_g, lambda_q1, lambda_k1, lambda_q2, lambda_k2, diff_subln_g, w_out, xattn_norm_g, mem_norm_g, xattn_w_q, xattn_w_kv, xattn_w_o, ffn2_norm_g, ffn2_w_gate, ffn2_w_up, ffn2_w_down, final_norm_g):
    b, s, d = x.shape
    depth = w_in.shape[0]
    assert depth == 1, "single-layer block only"
    tm, tq, hps, rc = _tiling(s)
    cos_t, sa_t, sb_t, cos_r, sin_r = _rope_lane_tables(s)
    xf = x.reshape(b * s, d)
    row = lambda v: v.astype(F32)[None, :]
    for layer in range(depth):
        lambda_init = 0.8 - 0.6 * math.exp(-0.3 * layer)
        w_kv = xattn_w_kv[layer]
        kt, vm = _kv_call(mem, row(mem_norm_g[layer]),
                          w_kv[:, :d].T.astype(BF16), w_kv[:, d:].astype(BF16))
        wg1, wu1, wd1 = _ff_chunks(ffn1_w_gate[layer], ffn1_w_up[layer], ffn1_w_down[layer])
        n_ssd_in = SSD_WIDTH + SSD_CONV_DIM + 2 * SSD_HEADS
        wl = w_in[layer]
        w_q = wl[:, n_ssd_in:n_ssd_in + DIFF_WIDTH]
        w_k = wl[:, n_ssd_in + DIFF_WIDTH:n_ssd_in + 2 * DIFF_WIDTH]
        w_v = wl[:, n_ssd_in + 2 * DIFF_WIDTH:]
        win = jnp.concatenate(
            [wl[:, :n_ssd_in], jnp.zeros((d, DT_PAD - 2 * SSD_HEADS), wl.dtype), w_k],
            axis=1).astype(BF16)
        x1, z, xbc, dt, qt, k, vt = _ffn_inproj_call(
            xf, row(ffn1_norm_g[layer]), wg1, wu1, wd1, row(mix_norm_g[layer]), win,
            w_q.T.astype(BF16), w_v.T.astype(BF16), cos_t, sa_t, sb_t, cos_r, sin_r, s, tm)
        cw = jnp.concatenate([conv_w[layer].astype(F32),
                              jnp.zeros((SUBLANES - SSD_CONV, SSD_CONV_DIM), F32)], axis=0)
        dtbias = _pad_lanes(jnp.concatenate([dt_bias_fwd[layer], dt_bias_bwd[layer]]), 0.0)
        alog = _pad_lanes(jnp.concatenate([A_log_fwd[layer], A_log_bwd[layer]]), -1e4)
        dexp = jnp.repeat(D_skip[layer].astype(F32), SSD_HEAD_DIM)[None, :]
        yf, yb = _ssd_call(xbc.reshape(b, s, SSD_CONV_DIM), dt.reshape(b, s, DT_PAD),
                           cw, row(conv_b[layer]), dtbias, alog, dexp, rc)
        lamv = jnp.stack([lambda_q1[layer], lambda_k1[layer],
                          lambda_q2[layer], lambda_k2[layer]]).astype(F32)
        yd = _diffattn_call(qt, k, vt, lamv, diff_subln_g[layer].astype(F32)[:, None],
                            lambda_init, s, tq, hps)
        wg2, wu2, wd2 = _ff_chunks(ffn2_w_gate[layer], ffn2_w_up[layer], ffn2_w_down[layer])
        gfin = row(final_norm_g)
        xf = _tail_call(x1, yf.reshape(b * s, SSD_WIDTH), yb.reshape(b * s, SSD_WIDTH), z,
                        yd, kt, vm,
                        row(ssd_norm_g[layer]), w_out[layer].astype(BF16), row(xattn_norm_g[layer]),
                        xattn_w_q[layer].astype(BF16), xattn_w_o[layer].astype(BF16),
                        row(ffn2_norm_g[layer]), wg2, wu2, wd2, gfin, s, tm)
    return xf.reshape(b, s, d)
```
